```python
import jax, jax.numpy as jnp
from jax import lax
import numpy as np

D_MODEL = 2048
BATCH = 1
SEQ = 16384
DEPTH = 2

HEAD_DIM = 128
N_MEM = 256
MEM_HEADS = 4
DIL_GROUPS = ((128, 1), (512, 4), (2048, 16))
HEADS_PER_GROUP = 4
DIL_HEADS = len(DIL_GROUPS) * HEADS_PER_GROUP
SB_HEADS = 12
BLOCK = 128
D_FF = 5632
CONV_WIDTH = 3
ROPE_THETA = 10000.0
EPS = 1e-6
NEG_INF = -1e30
N_A = DEPTH // 2
N_B = DEPTH - N_A

DIL_W = DIL_HEADS * HEAD_DIM
MEM_W = MEM_HEADS * HEAD_DIM
SB_W = SB_HEADS * HEAD_DIM
A_IN = 3 * DIL_W + MEM_W
B_IN = SB_W + MEM_W
A_OUT = HEADS_PER_GROUP * HEAD_DIM + MEM_W
B_OUT = SB_W + MEM_W

kernel_name = "yoco_dilated_stickbreaking_hybrid"


def _rmsnorm(x, g):
    xf = x.astype(jnp.float32)
    y = xf * lax.rsqrt(jnp.mean(xf * xf, axis=-1, keepdims=True) + EPS)
    return (y * g.astype(jnp.float32)).astype(x.dtype)


def _rope_tables(S):
    pos = jnp.arange(S, dtype=jnp.float32)
    inv = ROPE_THETA ** (-jnp.arange(0, HEAD_DIM, 2, dtype=jnp.float32) / HEAD_DIM)
    ang = pos[:, None] * inv[None, :]
    return jnp.cos(ang), jnp.sin(ang)


def _rope(x, cos, sin):
    half = x.shape[-1] // 2
    xf = x.astype(jnp.float32)
    x1, x2 = xf[..., :half], xf[..., half:]
    c, s = cos[None, :, None, :], sin[None, :, None, :]
    return jnp.concatenate([x1 * c - x2 * s, x2 * c + x1 * s], axis=-1).astype(x.dtype)


def _dilated_group(q, k, v, window, dilation):
    B, S, H, E = q.shape
    L = S // dilation
    nb = -(-L // BLOCK)
    Lp = nb * BLOCK
    span = window // dilation

    def to_blocks(t):
        t = t.reshape(B, L, dilation, H, E).transpose(0, 2, 3, 1, 4)
        t = jnp.pad(t, ((0, 0), (0, 0), (0, 0), (0, Lp - L), (0, 0)))
        return t.reshape(B, dilation, H, nb, BLOCK, E)

    def with_prev(t):
        prev = jnp.pad(t[:, :, :, :-1], ((0, 0), (0, 0), (0, 0), (1, 0), (0, 0), (0, 0)))
        return jnp.concatenate([prev, t], axis=4)

    qb, kb, vb = to_blocks(q), to_blocks(k), to_blocks(v)
    kk, vv = with_prev(kb), with_prev(vb)
    s = jnp.einsum('bdhnqe,bdhnke->bdhnqk', qb, kk,
                   preferred_element_type=jnp.float32) * (E ** -0.5)
    blk = jnp.arange(nb)[:, None, None]
    n_idx = blk * BLOCK + jnp.arange(BLOCK)[None, :, None]
    m_idx = (blk - 1) * BLOCK + jnp.arange(2 * BLOCK)[None, None, :]
    rel = n_idx - m_idx
    valid = (rel >= 0) & (rel <= span) & (m_idx >= 0)
    s = jnp.where(valid, s, NEG_INF)
    lse = jax.nn.logsumexp(s, axis=-1)
    p = jnp.exp(s - lse[..., None])
    o = jnp.einsum('bdhnqk,bdhnke->bdhnqe', p.astype(v.dtype), vv,
                   preferred_element_type=jnp.float32)
    o = o.reshape(B, dilation, H, Lp, E)[:, :, :, :L].transpose(0, 3, 1, 2, 4).reshape(B, S, H, E)
    lse = lse.reshape(B, dilation, H, Lp)[..., :L].transpose(0, 3, 1, 2).reshape(B, S, H)
    return o, lse


def _stick_breaking(q, k, v):
    B, S, H, E = q.shape
    nb = S // BLOCK
    scale = E ** -0.5
    qb = q.reshape(B, nb, BLOCK, H, E).transpose(1, 0, 3, 2, 4)
    kt = k.transpose(0, 2, 1, 3)
    vt = v.transpose(0, 2, 1, 3)
    key_pos = jnp.arange(S)

    def block(args):
        qblk, i = args
        z = jnp.einsum('bhqe,bhke->bhqk', qblk, kt,
                       preferred_element_type=jnp.float32) * scale
        q_pos = i * BLOCK + jnp.arange(BLOCK)
        causal = key_pos[None, :] < q_pos[:, None]
        log_keep = jnp.where(causal, jax.nn.log_sigmoid(-z), 0.0)
        later = lax.cumsum(log_keep, axis=3, reverse=True) - log_keep
        a = jnp.where(causal, jnp.exp(jax.nn.log_sigmoid(z) + later), 0.0)
        return jnp.einsum('bhqk,bhke->bhqe', a.astype(vt.dtype), vt,
                          preferred_element_type=jnp.float32).astype(q.dtype)

    o = lax.map(block, (qb, jnp.arange(nb)))
    return o.transpose(1, 0, 3, 2, 4).reshape(B, S, H, E)


def _memory_attention(qm, mem, g_mem, w_mem_kv):
    B, S, _ = qm.shape
    mkv = _rmsnorm(mem, g_mem) @ w_mem_kv
    mk, mv = jnp.split(mkv, 2, axis=-1)
    M = mem.shape[1]
    q = qm.reshape(B, S, MEM_HEADS, HEAD_DIM)
    mk = mk.reshape(B, M, MEM_HEADS, HEAD_DIM)
    mv = mv.reshape(B, M, MEM_HEADS, HEAD_DIM)
    s = jnp.einsum('bshe,bmhe->bhsm', q, mk, preferred_element_type=jnp.float32) * (HEAD_DIM ** -0.5)
    p = jax.nn.softmax(s, axis=-1)
    o = jnp.einsum('bhsm,bmhe->bshe', p.astype(mv.dtype), mv, preferred_element_type=jnp.float32)
    return o.astype(qm.dtype).reshape(B, S, MEM_W)


def _conv_ffn(h, w_up, w_conv, b_conv, w_down):
    S = h.shape[1]
    u = h @ w_up
    gate, val = jnp.split(u, 2, axis=-1)
    gp = jnp.pad(gate, ((0, 0), (CONV_WIDTH - 1, 0), (0, 0)))
    acc = b_conv + gp[:, CONV_WIDTH - 1:CONV_WIDTH - 1 + S] * w_conv[CONV_WIDTH - 1]
    for j in range(CONV_WIDTH - 1):
        acc = acc + gp[:, j:j + S] * w_conv[j]
    return (jax.nn.silu(acc) * val) @ w_down


def _finish_layer(x, heads, w_o, norms, w_up, w_conv, b_conv, w_down):
    x = x + _rmsnorm(heads @ w_o, norms[1])
    f = _conv_ffn(_rmsnorm(x, norms[2]), w_up, w_conv, b_conv, w_down)
    return x + _rmsnorm(f, norms[3])


def setup_inputs(seed: int = 0) -> dict:
    key = jax.random.key(seed)
    ks = jax.random.split(key, 16)
    D = D_MODEL

    def w(k, shape, fan_in):
        return jax.random.normal(k, shape, jnp.float32) * (fan_in ** -0.5)

    return {
        'x': jax.random.normal(ks[0], (BATCH, SEQ, D), jnp.float32),
        'mem': jax.random.normal(ks[1], (BATCH, N_MEM, D), jnp.float32),
        'norms': 1.0 + 0.05 * jax.random.normal(ks[2], (DEPTH, 5, D), jnp.float32),
        'w_in_a': w(ks[3], (N_A, D, A_IN), D),
        'w_o_a': w(ks[4], (N_A, A_OUT, D), A_OUT),
        'g_kv': 1.0 + 0.05 * jax.random.normal(ks[5], (D,), jnp.float32),
        'w_kv': w(ks[6], (D, 2 * SB_W), D),
        'w_in_b': w(ks[7], (N_B, D, B_IN), D),
        'w_o_b': w(ks[8], (N_B, B_OUT, D), B_OUT),
        'w_mem_kv': w(ks[9], (DEPTH, D, 2 * MEM_W), D),
        'w_up': w(ks[10], (DEPTH, D, 2 * D_FF), D),
        'w_conv': w(ks[11], (DEPTH, CONV_WIDTH, D_FF), CONV_WIDTH),
        'b_conv': 0.01 * jax.random.normal(ks[12], (DEPTH, D_FF), jnp.float32),
        'w_down': w(ks[13], (DEPTH, D_FF, D), D_FF),
    }


def reference(x, mem, norms, w_in_a, w_o_a, g_kv, w_kv, w_in_b, w_o_b,
              w_mem_kv, w_up, w_conv, b_conv, w_down):
    B, S, _ = x.shape
    cos, sin = _rope_tables(S)
    k_sh = v_sh = None
    for layer in range(DEPTH):
        nrm = norms[layer]
        if layer < N_A:
            h = _rmsnorm(x, nrm[0])
            proj = h @ w_in_a[layer]
            q, k, v, qm = jnp.split(proj, [DIL_W, 2 * DIL_W, 3 * DIL_W], axis=-1)
            q = _rope(q.reshape(B, S, DIL_HEADS, HEAD_DIM), cos, sin)
            k = _rope(k.reshape(B, S, DIL_HEADS, HEAD_DIM), cos, sin)
            v = v.reshape(B, S, DIL_HEADS, HEAD_DIM)
            outs, lses = [], []
            for gi, (window, dilation) in enumerate(DIL_GROUPS):
                sl = slice(gi * HEADS_PER_GROUP, (gi + 1) * HEADS_PER_GROUP)
                o_g, l_g = _dilated_group(q[:, :, sl], k[:, :, sl], v[:, :, sl], window, dilation)
                outs.append(o_g)
                lses.append(l_g)
            wts = jax.nn.softmax(jnp.stack(lses, axis=0), axis=0)
            o_dil = jnp.sum(wts[..., None] * jnp.stack(outs, axis=0), axis=0).astype(x.dtype)
            o_mem = _memory_attention(qm, mem, nrm[4], w_mem_kv[layer])
            heads = jnp.concatenate([o_dil.reshape(B, S, -1), o_mem], axis=-1)
            x = _finish_layer(x, heads, w_o_a[layer], nrm, w_up[layer], w_conv[layer],
                              b_conv[layer], w_down[layer])
        else:
            if layer == N_A:
                kv = _rmsnorm(x, g_kv) @ w_kv
                k_sh, v_sh = jnp.split(kv, 2, axis=-1)
                k_sh = k_sh.reshape(B, S, SB_HEADS, HEAD_DIM)
                v_sh = v_sh.reshape(B, S, SB_HEADS, HEAD_DIM)
            lb = layer - N_A
            h = _rmsnorm(x, nrm[0])
            proj = h @ w_in_b[lb]
            q, qm = jnp.split(proj, [SB_W], axis=-1)
            o_sb = _stick_breaking(q.reshape(B, S, SB_HEADS, HEAD_DIM), k_sh, v_sh)
            o_mem = _memory_attention(qm, mem, nrm[4], w_mem_kv[layer])
            heads = jnp.concatenate([o_sb.reshape(B, S, -1), o_mem], axis=-1)
            x = _finish_layer(x, heads, w_o_b[lb], nrm, w_up[layer], w_conv[layer],
                              b_conv[layer], w_down[layer])
    return x
```

```python
import functools

import jax
import jax.numpy as jnp
from jax import lax
from jax.experimental import pallas as pl
from jax.experimental.pallas import tpu as pltpu

HEAD_DIM = 128
N_MEM = 256
MEM_HEADS = 4
DIL_GROUPS = ((128, 1), (512, 4), (2048, 16))
HEADS_PER_GROUP = 4
DIL_HEADS = len(DIL_GROUPS) * HEADS_PER_GROUP
SB_HEADS = 12
BLOCK = 128
CONV_WIDTH = 3
ROPE_THETA = 10000.0
EPS = 1e-6
NEG_INF = -1e30

DIL_W = DIL_HEADS * HEAD_DIM
MEM_W = MEM_HEADS * HEAD_DIM
SB_W = SB_HEADS * HEAD_DIM
GROUP_W = HEADS_PER_GROUP * HEAD_DIM

F32 = jnp.float32
BF16 = jnp.bfloat16

VMEM_LIMIT_BYTES = 56 * 1024 * 1024
SUBLANE_BF16 = 16


def _params(*sem):
    return pltpu.CompilerParams(dimension_semantics=sem,
                                vmem_limit_bytes=VMEM_LIMIT_BYTES)


def _rms(xf, g):
    y = xf * lax.rsqrt(jnp.mean(xf * xf, axis=-1, keepdims=True) + EPS)
    return y * g


def _dot(a, b):
    return jnp.dot(a, b, preferred_element_type=F32)


def _dot_nt(a, b):
    return lax.dot_general(a, b, (((1,), (1,)), ((), ())),
                           preferred_element_type=F32)


def _norm_matmul_kernel(*refs, rope_tiles, heads_per_tile):
    if rope_tiles:
        x_ref, g_ref, w_ref, cos_ref, sin_ref, o_ref, h_ref = refs
    else:
        x_ref, g_ref, w_ref, o_ref, h_ref = refs
    j = pl.program_id(1)

    @pl.when(j == 0)
    def _():
        h_ref[...] = _rms(x_ref[...], g_ref[...]).astype(BF16)

    acc = _dot(h_ref[...], w_ref[...])

    if rope_tiles:
        @pl.when(j < rope_tiles)
        def _():
            c = cos_ref[...]
            s = sin_ref[...]
            for hh in range(heads_per_tile):
                sl = slice(hh * HEAD_DIM, (hh + 1) * HEAD_DIM)
                seg = acc[:, sl]
                rot = pltpu.roll(seg, HEAD_DIM // 2, 1)
                o_ref[:, sl] = (seg * c + rot * s).astype(o_ref.dtype)

        @pl.when(j >= rope_tiles)
        def _():
            o_ref[...] = acc.astype(o_ref.dtype)
    else:
        o_ref[...] = acc.astype(o_ref.dtype)


def _norm_matmul(x, g, w, *, tm, tn, rope_cols=0, cos2=None, sin2=None):
    m, k = x.shape
    n = w.shape[1]
    assert m % tm == 0 and n % tn == 0 and rope_cols % tn == 0
    in_specs = [
        pl.BlockSpec((tm, k), lambda i, j: (i, 0)),
        pl.BlockSpec((1, k), lambda i, j: (0, 0)),
        pl.BlockSpec((k, tn), lambda i, j: (0, j)),
    ]
    args = [x, g.reshape(1, k), w]
    if rope_cols:
        in_specs += [pl.BlockSpec((tm, HEAD_DIM), lambda i, j: (i, 0))] * 2
        args += [cos2, sin2]
    return pl.pallas_call(
        functools.partial(_norm_matmul_kernel, rope_tiles=rope_cols // tn,
                          heads_per_tile=tn // HEAD_DIM),
        grid=(m // tm, n // tn),
        in_specs=in_specs,
        out_specs=pl.BlockSpec((tm, tn), lambda i, j: (i, j)),
        out_shape=jax.ShapeDtypeStruct((m, n), BF16),
        scratch_shapes=[pltpu.VMEM((tm, k), BF16)],
        compiler_params=_params("parallel", "arbitrary"),
        name="norm_matmul",
    )(*args)


def _dilated_kernel(q_ref, kp_ref, kc_ref, vp_ref, vc_ref, o_ref, l_ref):
    n = pl.program_id(1)
    scale = HEAD_DIM ** -0.5
    qi = lax.broadcasted_iota(jnp.int32, (BLOCK, BLOCK), 0)
    kj = lax.broadcasted_iota(jnp.int32, (BLOCK, BLOCK), 1)
    valid_prev = (kj >= qi) & (n > 0)
    valid_cur = kj <= qi
    for hh in range(HEADS_PER_GROUP):
        sl = slice(hh * HEAD_DIM, (hh + 1) * HEAD_DIM)
        q = q_ref[:, sl]
        sp = jnp.where(valid_prev, _dot_nt(q, kp_ref[:, sl]) * scale, NEG_INF)
        sc = jnp.where(valid_cur, _dot_nt(q, kc_ref[:, sl]) * scale, NEG_INF)
        mx = jnp.maximum(jnp.max(sp, axis=-1, keepdims=True),
                         jnp.max(sc, axis=-1, keepdims=True))
        tot = (jnp.sum(jnp.exp(sp - mx), axis=-1, keepdims=True)
               + jnp.sum(jnp.exp(sc - mx), axis=-1, keepdims=True))
        lse = mx + jnp.log(tot)
        pp = jnp.exp(sp - lse).astype(BF16)
        pc = jnp.exp(sc - lse).astype(BF16)
        o_ref[:, sl] = _dot(pp, vp_ref[:, sl]) + _dot(pc, vc_ref[:, sl])
        l_ref[:, sl] = jnp.broadcast_to(lse, (BLOCK, HEAD_DIM))


def _dilated_group(proj, gi, dilation):
    s, width = proj.shape
    seq = s // dilation
    assert seq % BLOCK == 0
    nb = seq // BLOCK
    view = proj.reshape(seq, dilation * width)
    tiles_per_row = width // GROUP_W
    q_tile = gi
    k_tile = DIL_W // GROUP_W + gi
    v_tile = 2 * DIL_W // GROUP_W + gi

    def cur(tile):
        return pl.BlockSpec((BLOCK, GROUP_W),
                            lambda r, n: (n, r * tiles_per_row + tile))

    def prev(tile):
        return pl.BlockSpec(
            (BLOCK, GROUP_W),
            lambda r, n: (jnp.maximum(n - 1, 0), r * tiles_per_row + tile))

    out_spec = pl.BlockSpec((BLOCK, GROUP_W), lambda r, n: (n, r))
    o, l = pl.pallas_call(
        _dilated_kernel,
        grid=(dilation, nb),
        in_specs=[cur(q_tile), prev(k_tile), cur(k_tile), prev(v_tile),
                  cur(v_tile)],
        out_specs=[out_spec, out_spec],
        out_shape=[jax.ShapeDtypeStruct((seq, dilation * GROUP_W), F32)] * 2,
        compiler_params=_params("parallel", "arbitrary"),
        name=f"dilated_d{dilation}",
    )(view, view, view, view, view)
    return o.reshape(s, GROUP_W), l.reshape(s, GROUP_W)


def _mem_attn_kernel(q_ref, mkv_ref, o_ref):
    scale = HEAD_DIM ** -0.5
    for hh in range(MEM_HEADS):
        sl = slice(hh * HEAD_DIM, (hh + 1) * HEAD_DIM)
        vl = slice(MEM_W + hh * HEAD_DIM, MEM_W + (hh + 1) * HEAD_DIM)
        s = _dot_nt(q_ref[:, sl], mkv_ref[:, sl]) * scale
        e = jnp.exp(s - jnp.max(s, axis=-1, keepdims=True))
        p = e / jnp.sum(e, axis=-1, keepdims=True)
        o_ref[:, sl] = _dot(p.astype(BF16), mkv_ref[:, vl]).astype(o_ref.dtype)


def _mem_attn(proj, col_tile, mkv, *, tm):
    s = proj.shape[0]
    n_mem = mkv.shape[0]
    return pl.pallas_call(
        _mem_attn_kernel,
        grid=(s // tm,),
        in_specs=[pl.BlockSpec((tm, MEM_W), lambda i: (i, col_tile)),
                  pl.BlockSpec((n_mem, 2 * MEM_W), lambda i: (0, 0))],
        out_specs=pl.BlockSpec((tm, MEM_W), lambda i: (i, 0)),
        out_shape=jax.ShapeDtypeStruct((s, MEM_W), BF16),
        compiler_params=_params("parallel"),
        name="mem_attn",
    )(proj, mkv)


def _outproj_merge_kernel(o1, o2, o3, l1, l2, l3, om_ref, x_ref, w_ref, g_ref,
                          out_ref):
    a1, a2, a3 = l1[...], l2[...], l3[...]
    mx = jnp.maximum(jnp.maximum(a1, a2), a3)
    e1, e2, e3 = jnp.exp(a1 - mx), jnp.exp(a2 - mx), jnp.exp(a3 - mx)
    den = e1 + e2 + e3
    o_dil = (e1 / den) * o1[...] + (e2 / den) * o2[...] + (e3 / den) * o3[...]
    heads = jnp.concatenate([o_dil.astype(BF16), om_ref[...]], axis=1)
    y = _dot(heads, w_ref[...])
    out_ref[...] = x_ref[...] + _rms(y, g_ref[...])


def _outproj_merge(outs, lses, o_mem, x, w_o, g, *, tm):
    s, d = x.shape
    row = lambda w: pl.BlockSpec((tm, w), lambda i: (i, 0))
    full = lambda a: pl.BlockSpec(a.shape, lambda i: (0, 0))
    g2 = g.reshape(1, d)
    return pl.pallas_call(
        _outproj_merge_kernel,
        grid=(s // tm,),
        in_specs=[row(GROUP_W)] * 6 + [row(MEM_W), row(d), full(w_o), full(g2)],
        out_specs=row(d),
        out_shape=jax.ShapeDtypeStruct((s, d), F32),
        compiler_params=_params("parallel"),
        name="outproj_merge",
    )(*outs, *lses, o_mem, x, w_o, g2)


def _outproj_kernel(oa_ref, om_ref, x_ref, w_ref, g_ref, out_ref):
    heads = jnp.concatenate([oa_ref[...], om_ref[...]], axis=1)
    y = _dot(heads, w_ref[...])
    out_ref[...] = x_ref[...] + _rms(y, g_ref[...])


def _outproj(o_attn, o_mem, x, w_o, g, *, tm):
    s, d = x.shape
    row = lambda w: pl.BlockSpec((tm, w), lambda i: (i, 0))
    full = lambda a: pl.BlockSpec(a.shape, lambda i: (0, 0))
    g2 = g.reshape(1, d)
    return pl.pallas_call(
        _outproj_kernel,
        grid=(s // tm,),
        in_specs=[row(o_attn.shape[1]), row(MEM_W), row(d), full(w_o),
                  full(g2)],
        out_specs=row(d),
        out_shape=jax.ShapeDtypeStruct((s, d), F32),
        compiler_params=_params("parallel"),
        name="outproj",
    )(o_attn, o_mem, x, w_o, g2)


def _ffn_kernel(x_ref, xh_ref, gin_ref, wg_ref, wv_ref, wc_ref, bc_ref, wd_ref,
                gout_ref, o_ref, hn_ref, hh_ref, gbuf_ref, *, tm):
    i = pl.program_id(0)
    j = pl.program_id(1)
    halo = SUBLANE_BF16

    @pl.when(j == 0)
    def _():
        hn_ref[...] = _rms(x_ref[...], gin_ref[...]).astype(BF16)
        hh_ref[...] = _rms(xh_ref[...], gin_ref[...]).astype(BF16)
        o_ref[...] = jnp.zeros_like(o_ref)

    wg = wg_ref[...]
    gate = _dot(hn_ref[...], wg)
    val = _dot(hn_ref[...], wv_ref[...])
    gate_halo = jnp.where(i > 0, _dot(hh_ref[...], wg), 0.0)
    gbuf_ref[0:halo, :] = gate_halo
    gbuf_ref[halo:halo + tm, :] = gate
    g_m1 = gbuf_ref[pl.ds(halo - 1, tm), :]
    g_m2 = gbuf_ref[pl.ds(halo - 2, tm), :]
    acc = bc_ref[...] + gate * wc_ref[2:3, :]
    acc = acc + g_m2 * wc_ref[0:1, :]
    acc = acc + g_m1 * wc_ref[1:2, :]
    act = (acc * (1.0 / (1.0 + jnp.exp(-acc)))) * val
    o_ref[...] += _dot(act.astype(BF16), wd_ref[...])

    @pl.when(j == pl.num_programs(1) - 1)
    def _():
        o_ref[...] = x_ref[...] + _rms(o_ref[...], gout_ref[...])


def _ffn(x, g_in, w_up, w_conv, b_conv, w_down, g_out, *, tm, tf):
    s, d = x.shape
    d_ff = w_down.shape[0]
    assert s % tm == 0 and d_ff % tf == 0 and tm % SUBLANE_BF16 == 0
    nf = d_ff // tf
    halo = SUBLANE_BF16
    return pl.pallas_call(
        functools.partial(_ffn_kernel, tm=tm),
        grid=(s // tm, nf),
        in_specs=[
            pl.BlockSpec((tm, d), lambda i, j: (i, 0)),
            pl.BlockSpec((halo, d),
                         lambda i, j: (jnp.maximum(i * (tm // halo) - 1, 0), 0)),
            pl.BlockSpec((1, d), lambda i, j: (0, 0)),
            pl.BlockSpec((d, tf), lambda i, j: (0, j)),
            pl.BlockSpec((d, tf), lambda i, j: (0, nf + j)),
            pl.BlockSpec((CONV_WIDTH, tf), lambda i, j: (0, j)),
            pl.BlockSpec((1, tf), lambda i, j: (0, j)),
            pl.BlockSpec((tf, d), lambda i, j: (j, 0)),
            pl.BlockSpec((1, d), lambda i, j: (0, 0)),
        ],
        out_specs=pl.BlockSpec((tm, d), lambda i, j: (i, 0)),
        out_shape=jax.ShapeDtypeStruct((s, d), F32),
        scratch_shapes=[pltpu.VMEM((tm, d), BF16),
                        pltpu.VMEM((halo, d), BF16),
                        pltpu.VMEM((halo + tm, tf), F32)],
        compiler_params=_params("parallel", "arbitrary"),
        name="conv_ffn",
    )(x, x, g_in.reshape(1, d), w_up, w_up, w_conv, b_conv.reshape(1, d_ff),
      w_down, g_out.reshape(1, d))


def _sb_kernel(q_ref, k_ref, v_ref, o_ref, acc_ref, carry_ref, *, tq):
    i = pl.program_id(1)
    scale = HEAD_DIM ** -0.5
    half = tq // 2
    q = q_ref[...]

    rr = lax.broadcasted_iota(jnp.int32, (2 * half, 2 * half), 0)
    cc = lax.broadcasted_iota(jnp.int32, (2 * half, 2 * half), 1)
    tri = jnp.where((cc >= half) | ((rr & (half - 1)) > cc), 1.0, 0.0).astype(BF16)

    def tile(kstart, causal):
        kb = k_ref[pl.ds(kstart, tq), :]
        vb = v_ref[pl.ds(kstart, tq), :]
        z = _dot_nt(q, kb) * scale
        soft = jnp.log1p(jnp.exp(-jnp.abs(z)))
        log_sig = jnp.minimum(z, 0.0) - soft
        log_keep = -jnp.maximum(z, 0.0) - soft
        if causal is not None:
            log_keep = jnp.where(causal, log_keep, 0.0)
        hi = log_keep.astype(BF16)
        lo = (log_keep - hi.astype(F32)).astype(BF16)
        res_b = _dot(jnp.concatenate([hi[:, half:], lo[:, half:]], axis=1), tri)
        res_a = _dot(jnp.concatenate([hi[:, :half], lo[:, :half]], axis=1), tri)
        carry = carry_ref[...]
        carry_a = carry + res_b[:, half:]
        a_b = jnp.exp(log_sig[:, half:] + (carry + res_b[:, :half]))
        a_a = jnp.exp(log_sig[:, :half] + (carry_a + res_a[:, :half]))
        a = jnp.concatenate([a_a, a_b], axis=1)
        if causal is not None:
            a = jnp.where(causal, a, 0.0)
        acc_ref[...] += _dot(a.astype(BF16), vb)
        carry_ref[...] = carry_a + res_a[:, half:]

    acc_ref[...] = jnp.zeros_like(acc_ref)
    carry_ref[...] = jnp.zeros_like(carry_ref)
    qi = lax.broadcasted_iota(jnp.int32, (tq, tq), 0)
    kj = lax.broadcasted_iota(jnp.int32, (tq, tq), 1)
    tile(pl.multiple_of(i * tq, tq), kj < qi)

    def body(step, c):
        tile(pl.multiple_of((i - 1 - step) * tq, tq), None)
        return c

    lax.fori_loop(0, i, body, 0)
    o_ref[...] = acc_ref[...].astype(o_ref.dtype)


def _stick_breaking(proj, kv, *, tq):
    s = proj.shape[0]
    assert tq == 2 * HEAD_DIM and s % tq == 0
    return pl.pallas_call(
        functools.partial(_sb_kernel, tq=tq),
        grid=(SB_HEADS, s // tq),
        in_specs=[pl.BlockSpec((tq, HEAD_DIM), lambda h, i: (i, h)),
                  pl.BlockSpec((s, HEAD_DIM), lambda h, i: (0, h)),
                  pl.BlockSpec((s, HEAD_DIM), lambda h, i: (0, SB_HEADS + h))],
        out_specs=pl.BlockSpec((tq, HEAD_DIM), lambda h, i: (i, h)),
        out_shape=jax.ShapeDtypeStruct((s, SB_W), BF16),
        scratch_shapes=[pltpu.VMEM((tq, HEAD_DIM), F32),
                        pltpu.VMEM((tq, HEAD_DIM), F32)],
        compiler_params=_params("parallel", "arbitrary"),
        name="stick_breaking",
    )(proj, kv, kv)


def _rope_tables(s):
    pos = jnp.arange(s, dtype=F32)
    inv = ROPE_THETA ** (-jnp.arange(0, HEAD_DIM, 2, dtype=F32) / HEAD_DIM)
    ang = pos[:, None] * inv[None, :]
    cos, sin = jnp.cos(ang), jnp.sin(ang)
    return (jnp.concatenate([cos, cos], axis=-1),
            jnp.concatenate([-sin, sin], axis=-1))


def kernel(x, mem, norms, w_in_a, w_o_a, g_kv, w_kv, w_in_b, w_o_b, w_mem_kv,
           w_up, w_conv, b_conv, w_down):
    b, s, d = x.shape
    assert b == 1 and norms.shape[0] == 2
    xs = x.reshape(s, d)
    mems = mem.reshape(mem.shape[1], d)
    bf = lambda a: a.astype(BF16)
    tm = min(1024, s)

    nrm = norms[0]
    cos2, sin2 = _rope_tables(s)
    proj = _norm_matmul(xs, nrm[0], bf(w_in_a[0]), tm=tm, tn=GROUP_W,
                        rope_cols=2 * DIL_W, cos2=cos2, sin2=sin2)
    mkv = _norm_matmul(mems, nrm[4], bf(w_mem_kv[0]), tm=mems.shape[0],
                       tn=GROUP_W)
    outs, lses = [], []
    for gi, (_, dilation) in enumerate(DIL_GROUPS):
        o_g, l_g = _dilated_group(proj, gi, dilation)
        outs.append(o_g)
        lses.append(l_g)
    o_mem = _mem_attn(proj, 3 * DIL_W // MEM_W, mkv, tm=min(512, s))
    xs = _outproj_merge(outs, lses, o_mem, xs, bf(w_o_a[0]), nrm[1],
                        tm=min(512, s))
    xs = _ffn(xs, nrm[2], bf(w_up[0]), w_conv[0], b_conv[0], bf(w_down[0]),
              nrm[3], tm=min(512, s), tf=512)

    nrm = norms[1]
    kv = _norm_matmul(xs, g_kv, bf(w_kv), tm=tm, tn=GROUP_W)
    proj = _norm_matmul(xs, nrm[0], bf(w_in_b[0]), tm=tm, tn=GROUP_W)
    mkv = _norm_matmul(mems, nrm[4], bf(w_mem_kv[1]), tm=mems.shape[0],
                       tn=GROUP_W)
    o_sb = _stick_breaking(proj, kv, tq=2 * HEAD_DIM)
    o_mem = _mem_attn(proj, SB_W // MEM_W, mkv, tm=min(512, s))
    xs = _outproj(o_sb, o_mem, xs, bf(w_o_b[0]), nrm[1], tm=min(512, s))
    xs = _ffn(xs, nrm[2], bf(w_up[1]), w_conv[1], b_conv[1], bf(w_down[1]),
              nrm[3], tm=min(512, s), tf=512)
    return xs.reshape(b, s, d)
```

```python
import functools

import jax
import jax.numpy as jnp
from jax import lax
from jax.experimental import pallas as pl
from jax.experimental.pallas import tpu as pltpu

HEAD_DIM = 128
N_MEM = 256
MEM_HEADS = 4
DIL_GROUPS = ((128, 1), (512, 4), (2048, 16))
HEADS_PER_GROUP = 4
DIL_HEADS = len(DIL_GROUPS) * HEADS_PER_GROUP
SB_HEADS = 12
BLOCK = 128
CONV_WIDTH = 3
ROPE_THETA = 10000.0
EPS = 1e-6
NEG_INF = -1e30
EXP_UNDERFLOW_F32 = -105.0

DIL_W = DIL_HEADS * HEAD_DIM
MEM_W = MEM_HEADS * HEAD_DIM
SB_W = SB_HEADS * HEAD_DIM
GROUP_W = HEADS_PER_GROUP * HEAD_DIM

F32 = jnp.float32
BF16 = jnp.bfloat16

VMEM_LIMIT_BYTES = 56 * 1024 * 1024
SUBLANE_BF16 = 16


def _params(*sem):
    return pltpu.CompilerParams(dimension_semantics=sem,
                                vmem_limit_bytes=VMEM_LIMIT_BYTES)


def _rms(xf, g):
    y = xf * lax.rsqrt(jnp.mean(xf * xf, axis=-1, keepdims=True) + EPS)
    return y * g


def _dot(a, b):
    return jnp.dot(a, b, preferred_element_type=F32)


def _dot_nt(a, b):
    return lax.dot_general(a, b, (((1,), (1,)), ((), ())),
                           preferred_element_type=F32)


def _norm_matmul_kernel(*refs, rope_tiles, heads_per_tile):
    if rope_tiles:
        x_ref, g_ref, w_ref, cos_ref, sin_ref, o_ref, h_ref = refs
    else:
        x_ref, g_ref, w_ref, o_ref, h_ref = refs
    j = pl.program_id(1)

    @pl.when(j == 0)
    def _():
        h_ref[...] = _rms(x_ref[...], g_ref[...]).astype(BF16)

    acc = _dot(h_ref[...], w_ref[...])

    if rope_tiles:
        @pl.when(j < rope_tiles)
        def _():
            c = cos_ref[...]
            s = sin_ref[...]
            for hh in range(heads_per_tile):
                sl = slice(hh * HEAD_DIM, (hh + 1) * HEAD_DIM)
                seg = acc[:, sl]
                rot = pltpu.roll(seg, HEAD_DIM // 2, 1)
                o_ref[:, sl] = (seg * c + rot * s).astype(o_ref.dtype)

        @pl.when(j >= rope_tiles)
        def _():
            o_ref[...] = acc.astype(o_ref.dtype)
    else:
        o_ref[...] = acc.astype(o_ref.dtype)


def _norm_matmul(x, g, w, *, tm, tn, rope_cols=0, cos2=None, sin2=None):
    m, k = x.shape
    n = w.shape[1]
    assert m % tm == 0 and n % tn == 0 and rope_cols % tn == 0
    in_specs = [
        pl.BlockSpec((tm, k), lambda i, j: (i, 0)),
        pl.BlockSpec((1, k), lambda i, j: (0, 0)),
        pl.BlockSpec((k, tn), lambda i, j: (0, j)),
    ]
    args = [x, g.reshape(1, k), w]
    if rope_cols:
        in_specs += [pl.BlockSpec((tm, HEAD_DIM), lambda i, j: (i, 0))] * 2
        args += [cos2, sin2]
    return pl.pallas_call(
        functools.partial(_norm_matmul_kernel, rope_tiles=rope_cols // tn,
                          heads_per_tile=tn // HEAD_DIM),
        grid=(m // tm, n // tn),
        in_specs=in_specs,
        out_specs=pl.BlockSpec((tm, tn), lambda i, j: (i, j)),
        out_shape=jax.ShapeDtypeStruct((m, n), BF16),
        scratch_shapes=[pltpu.VMEM((tm, k), BF16)],
        compiler_params=_params("parallel", "arbitrary"),
        name="norm_matmul",
    )(*args)


def _dilated_kernel(q_ref, kp_ref, kc_ref, vp_ref, vc_ref, o_ref, l_ref):
    n = pl.program_id(1)
    scale = HEAD_DIM ** -0.5
    qi = lax.broadcasted_iota(jnp.int32, (BLOCK, BLOCK), 0)
    kj = lax.broadcasted_iota(jnp.int32, (BLOCK, BLOCK), 1)
    valid_prev = (kj >= qi) & (n > 0)
    valid_cur = kj <= qi
    for hh in range(HEADS_PER_GROUP):
        sl = slice(hh * HEAD_DIM, (hh + 1) * HEAD_DIM)
        q = q_ref[:, sl]
        sp = jnp.where(valid_prev, _dot_nt(q, kp_ref[:, sl]) * scale, NEG_INF)
        sc = jnp.where(valid_cur, _dot_nt(q, kc_ref[:, sl]) * scale, NEG_INF)
        mx = jnp.maximum(jnp.max(sp, axis=-1, keepdims=True),
                         jnp.max(sc, axis=-1, keepdims=True))
        tot = (jnp.sum(jnp.exp(sp - mx), axis=-1, keepdims=True)
               + jnp.sum(jnp.exp(sc - mx), axis=-1, keepdims=True))
        lse = mx + jnp.log(tot)
        pp = jnp.exp(sp - lse).astype(BF16)
        pc = jnp.exp(sc - lse).astype(BF16)
        o_ref[:, sl] = _dot(pp, vp_ref[:, sl]) + _dot(pc, vc_ref[:, sl])
        l_ref[:, sl] = jnp.broadcast_to(lse, (BLOCK, HEAD_DIM))


def _dilated_group(proj, gi, dilation):
    s, width = proj.shape
    seq = s // dilation
    assert seq % BLOCK == 0
    nb = seq // BLOCK
    view = proj.reshape(seq, dilation * width)
    tiles_per_row = width // GROUP_W
    q_tile = gi
    k_tile = DIL_W // GROUP_W + gi
    v_tile = 2 * DIL_W // GROUP_W + gi

    def cur(tile):
        return pl.BlockSpec((BLOCK, GROUP_W),
                            lambda r, n: (n, r * tiles_per_row + tile))

    def prev(tile):
        return pl.BlockSpec(
            (BLOCK, GROUP_W),
            lambda r, n: (jnp.maximum(n - 1, 0), r * tiles_per_row + tile))

    out_spec = pl.BlockSpec((BLOCK, GROUP_W), lambda r, n: (n, r))
    o, l = pl.pallas_call(
        _dilated_kernel,
        grid=(dilation, nb),
        in_specs=[cur(q_tile), prev(k_tile), cur(k_tile), prev(v_tile),
                  cur(v_tile)],
        out_specs=[out_spec, out_spec],
        out_shape=[jax.ShapeDtypeStruct((seq, dilation * GROUP_W), F32)] * 2,
        compiler_params=_params("parallel", "arbitrary"),
        name=f"dilated_d{dilation}",
    )(view, view, view, view, view)
    return o.reshape(s, GROUP_W), l.reshape(s, GROUP_W)


def _mem_attn_kernel(q_ref, mkv_ref, o_ref):
    scale = HEAD_DIM ** -0.5
    for hh in range(MEM_HEADS):
        sl = slice(hh * HEAD_DIM, (hh + 1) * HEAD_DIM)
        vl = slice(MEM_W + hh * HEAD_DIM, MEM_W + (hh + 1) * HEAD_DIM)
        s = _dot_nt(q_ref[:, sl], mkv_ref[:, sl]) * scale
        e = jnp.exp(s - jnp.max(s, axis=-1, keepdims=True))
        p = e / jnp.sum(e, axis=-1, keepdims=True)
        o_ref[:, sl] = _dot(p.astype(BF16), mkv_ref[:, vl]).astype(o_ref.dtype)


def _mem_attn(proj, col_tile, mkv, *, tm):
    s = proj.shape[0]
    n_mem = mkv.shape[0]
    return pl.pallas_call(
        _mem_attn_kernel,
        grid=(s // tm,),
        in_specs=[pl.BlockSpec((tm, MEM_W), lambda i: (i, col_tile)),
                  pl.BlockSpec((n_mem, 2 * MEM_W), lambda i: (0, 0))],
        out_specs=pl.BlockSpec((tm, MEM_W), lambda i: (i, 0)),
        out_shape=jax.ShapeDtypeStruct((s, MEM_W), BF16),
        compiler_params=_params("parallel"),
        name="mem_attn",
    )(proj, mkv)


def _outproj_merge_kernel(o1, o2, o3, l1, l2, l3, om_ref, x_ref, w_ref, g_ref,
                          out_ref):
    a1, a2, a3 = l1[...], l2[...], l3[...]
    mx = jnp.maximum(jnp.maximum(a1, a2), a3)
    e1, e2, e3 = jnp.exp(a1 - mx), jnp.exp(a2 - mx), jnp.exp(a3 - mx)
    den = e1 + e2 + e3
    o_dil = (e1 / den) * o1[...] + (e2 / den) * o2[...] + (e3 / den) * o3[...]
    heads = jnp.concatenate([o_dil.astype(BF16), om_ref[...]], axis=1)
    y = _dot(heads, w_ref[...])
    out_ref[...] = x_ref[...] + _rms(y, g_ref[...])


def _outproj_merge(outs, lses, o_mem, x, w_o, g, *, tm):
    s, d = x.shape
    row = lambda w: pl.BlockSpec((tm, w), lambda i: (i, 0))
    full = lambda a: pl.BlockSpec(a.shape, lambda i: (0, 0))
    g2 = g.reshape(1, d)
    return pl.pallas_call(
        _outproj_merge_kernel,
        grid=(s // tm,),
        in_specs=[row(GROUP_W)] * 6 + [row(MEM_W), row(d), full(w_o), full(g2)],
        out_specs=row(d),
        out_shape=jax.ShapeDtypeStruct((s, d), F32),
        compiler_params=_params("parallel"),
        name="outproj_merge",
    )(*outs, *lses, o_mem, x, w_o, g2)


def _outproj_kernel(oa_ref, om_ref, x_ref, w_ref, g_ref, out_ref):
    heads = jnp.concatenate([oa_ref[...], om_ref[...]], axis=1)
    y = _dot(heads, w_ref[...])
    out_ref[...] = x_ref[...] + _rms(y, g_ref[...])


def _outproj(o_attn, o_mem, x, w_o, g, *, tm):
    s, d = x.shape
    row = lambda w: pl.BlockSpec((tm, w), lambda i: (i, 0))
    full = lambda a: pl.BlockSpec(a.shape, lambda i: (0, 0))
    g2 = g.reshape(1, d)
    return pl.pallas_call(
        _outproj_kernel,
        grid=(s // tm,),
        in_specs=[row(o_attn.shape[1]), row(MEM_W), row(d), full(w_o),
                  full(g2)],
        out_specs=row(d),
        out_shape=jax.ShapeDtypeStruct((s, d), F32),
        compiler_params=_params("parallel"),
        name="outproj",
    )(o_attn, o_mem, x, w_o, g2)


def _ffn_kernel(x_ref, xh_ref, gin_ref, wg_ref, wv_ref, wc_ref, bc_ref, wd_ref,
                gout_ref, o_ref, hn_ref, hh_ref, gbuf_ref, *, tm):
    i = pl.program_id(0)
    j = pl.program_id(1)
    halo = SUBLANE_BF16

    @pl.when(j == 0)
    def _():
        hn_ref[...] = _rms(x_ref[...], gin_ref[...]).astype(BF16)
        hh_ref[...] = _rms(xh_ref[...], gin_ref[...]).astype(BF16)
        o_ref[...] = jnp.zeros_like(o_ref)

    wg = wg_ref[...]
    gate = _dot(hn_ref[...], wg)
    val = _dot(hn_ref[...], wv_ref[...])
    gate_halo = jnp.where(i > 0, _dot(hh_ref[...], wg), 0.0)
    gbuf_ref[0:halo, :] = gate_halo
    gbuf_ref[halo:halo + tm, :] = gate
    g_m1 = gbuf_ref[pl.ds(halo - 1, tm), :]
    g_m2 = gbuf_ref[pl.ds(halo - 2, tm), :]
    acc = bc_ref[...] + gate * wc_ref[2:3, :]
    acc = acc + g_m2 * wc_ref[0:1, :]
    acc = acc + g_m1 * wc_ref[1:2, :]
    act = (acc * (1.0 / (1.0 + jnp.exp(-acc)))) * val
    o_ref[...] += _dot(act.astype(BF16), wd_ref[...])

    @pl.when(j == pl.num_programs(1) - 1)
    def _():
        o_ref[...] = x_ref[...] + _rms(o_ref[...], gout_ref[...])


def _ffn(x, g_in, w_up, w_conv, b_conv, w_down, g_out, *, tm, tf):
    s, d = x.shape
    d_ff = w_down.shape[0]
    assert s % tm == 0 and d_ff % tf == 0 and tm % SUBLANE_BF16 == 0
    nf = d_ff // tf
    halo = SUBLANE_BF16
    return pl.pallas_call(
        functools.partial(_ffn_kernel, tm=tm),
        grid=(s // tm, nf),
        in_specs=[
            pl.BlockSpec((tm, d), lambda i, j: (i, 0)),
            pl.BlockSpec((halo, d),
                         lambda i, j: (jnp.maximum(i * (tm // halo) - 1, 0), 0)),
            pl.BlockSpec((1, d), lambda i, j: (0, 0)),
            pl.BlockSpec((d, tf), lambda i, j: (0, j)),
            pl.BlockSpec((d, tf), lambda i, j: (0, nf + j)),
            pl.BlockSpec((CONV_WIDTH, tf), lambda i, j: (0, j)),
            pl.BlockSpec((1, tf), lambda i, j: (0, j)),
            pl.BlockSpec((tf, d), lambda i, j: (j, 0)),
            pl.BlockSpec((1, d), lambda i, j: (0, 0)),
        ],
        out_specs=pl.BlockSpec((tm, d), lambda i, j: (i, 0)),
        out_shape=jax.ShapeDtypeStruct((s, d), F32),
        scratch_shapes=[pltpu.VMEM((tm, d), BF16),
                        pltpu.VMEM((halo, d), BF16),
                        pltpu.VMEM((halo + tm, tf), F32)],
        compiler_params=_params("parallel", "arbitrary"),
        name="conv_ffn",
    )(x, x, g_in.reshape(1, d), w_up, w_up, w_conv, b_conv.reshape(1, d_ff),
      w_down, g_out.reshape(1, d))


def _sb_kernel(q_ref, k_ref, v_ref, o_ref, acc_ref, carry_ref, *, tq):
    i = pl.program_id(1)
    scale = HEAD_DIM ** -0.5
    half = tq // 2
    q = q_ref[...]

    rr = lax.broadcasted_iota(jnp.int32, (2 * half, 2 * half), 0)
    cc = lax.broadcasted_iota(jnp.int32, (2 * half, 2 * half), 1)
    tri = jnp.where((cc >= half) | ((rr & (half - 1)) > cc), 1.0, 0.0).astype(BF16)

    def tile(kstart, causal):
        kb = k_ref[pl.ds(kstart, tq), :]
        vb = v_ref[pl.ds(kstart, tq), :]
        z = _dot_nt(q, kb) * scale
        soft = jnp.log1p(jnp.exp(-jnp.abs(z)))
        log_sig = jnp.minimum(z, 0.0) - soft
        log_keep = -jnp.maximum(z, 0.0) - soft
        if causal is not None:
            log_keep = jnp.where(causal, log_keep, 0.0)
        hi = log_keep.astype(BF16)
        lo = (log_keep - hi.astype(F32)).astype(BF16)
        res_b = _dot(jnp.concatenate([hi[:, half:], lo[:, half:]], axis=1), tri)
        res_a = _dot(jnp.concatenate([hi[:, :half], lo[:, :half]], axis=1), tri)
        carry = carry_ref[...]
        carry_a = carry + res_b[:, half:]
        a_b = jnp.exp(log_sig[:, half:] + (carry + res_b[:, :half]))
        a_a = jnp.exp(log_sig[:, :half] + (carry_a + res_a[:, :half]))
        a = jnp.concatenate([a_a, a_b], axis=1)
        if causal is not None:
            a = jnp.where(causal, a, 0.0)
        acc_ref[...] += _dot(a.astype(BF16), vb)
        carry_ref[...] = carry_a + res_a[:, half:]

    acc_ref[...] = jnp.zeros_like(acc_ref)
    carry_ref[...] = jnp.zeros_like(carry_ref)
    qi = lax.broadcasted_iota(jnp.int32, (tq, tq), 0)
    kj = lax.broadcasted_iota(jnp.int32, (tq, tq), 1)
    tile(pl.multiple_of(i * tq, tq), kj < qi)

    def live():
        return jnp.max(carry_ref[...]) > EXP_UNDERFLOW_F32

    def cond(c):
        step, alive = c
        return (step < i) & alive

    def body(c):
        step, _ = c
        tile(pl.multiple_of((i - 1 - step) * tq, tq), None)
        return step + 1, live()

    lax.while_loop(cond, body, (jnp.int32(0), live()))
    o_ref[...] = acc_ref[...].astype(o_ref.dtype)


def _stick_breaking(proj, kv, *, tq):
    s = proj.shape[0]
    assert tq == 2 * HEAD_DIM and s % tq == 0
    return pl.pallas_call(
        functools.partial(_sb_kernel, tq=tq),
        grid=(SB_HEADS, s // tq),
        in_specs=[pl.BlockSpec((tq, HEAD_DIM), lambda h, i: (i, h)),
                  pl.BlockSpec((s, HEAD_DIM), lambda h, i: (0, h)),
                  pl.BlockSpec((s, HEAD_DIM), lambda h, i: (0, SB_HEADS + h))],
        out_specs=pl.BlockSpec((tq, HEAD_DIM), lambda h, i: (i, h)),
        out_shape=jax.ShapeDtypeStruct((s, SB_W), BF16),
        scratch_shapes=[pltpu.VMEM((tq, HEAD_DIM), F32),
                        pltpu.VMEM((tq, HEAD_DIM), F32)],
        compiler_params=_params("parallel", "arbitrary"),
        name="stick_breaking",
    )(proj, kv, kv)


def _rope_tables(s):
    pos = jnp.arange(s, dtype=F32)
    inv = ROPE_THETA ** (-jnp.arange(0, HEAD_DIM, 2, dtype=F32) / HEAD_DIM)
    ang = pos[:, None] * inv[None, :]
    cos, sin = jnp.cos(ang), jnp.sin(ang)
    return (jnp.concatenate([cos, cos], axis=-1),
            jnp.concatenate([-sin, sin], axis=-1))


def kernel(x, mem, norms, w_in_a, w_o_a, g_kv, w_kv, w_in_b, w_o_b, w_mem_kv,
           w_up, w_conv, b_conv, w_down):
    b, s, d = x.shape
    assert b == 1 and norms.shape[0] == 2
    xs = x.reshape(s, d)
    mems = mem.reshape(mem.shape[1], d)
    bf = lambda a: a.astype(BF16)
    tm = min(1024, s)

    nrm = norms[0]
    cos2, sin2 = _rope_tables(s)
    proj = _norm_matmul(xs, nrm[0], bf(w_in_a[0]), tm=tm, tn=GROUP_W,
                        rope_cols=2 * DIL_W, cos2=cos2, sin2=sin2)
    mkv = _norm_matmul(mems, nrm[4], bf(w_mem_kv[0]), tm=mems.shape[0],
                       tn=GROUP_W)
    outs, lses = [], []
    for gi, (_, dilation) in enumerate(DIL_GROUPS):
        o_g, l_g = _dilated_group(proj, gi, dilation)
        outs.append(o_g)
        lses.append(l_g)
    o_mem = _mem_attn(proj, 3 * DIL_W // MEM_W, mkv, tm=min(512, s))
    xs = _outproj_merge(outs, lses, o_mem, xs, bf(w_o_a[0]), nrm[1],
                        tm=min(512, s))
    xs = _ffn(xs, nrm[2], bf(w_up[0]), w_conv[0], b_conv[0], bf(w_down[0]),
              nrm[3], tm=min(512, s), tf=512)

    nrm = norms[1]
    kv = _norm_matmul(xs, g_kv, bf(w_kv), tm=tm, tn=GROUP_W)
    proj = _norm_matmul(xs, nrm[0], bf(w_in_b[0]), tm=tm, tn=GROUP_W)
    mkv = _norm_matmul(mems, nrm[4], bf(w_mem_kv[1]), tm=mems.shape[0],
                       tn=GROUP_W)
    o_sb = _stick_breaking(proj, kv, tq=2 * HEAD_DIM)
    o_mem = _mem_attn(proj, SB_W // MEM_W, mkv, tm=min(512, s))
    xs = _outproj(o_sb, o_mem, xs, bf(w_o_b[0]), nrm[1], tm=min(512, s))
    xs = _ffn(xs, nrm[2], bf(w_up[1]), w_conv[1], b_conv[1], bf(w_down[1]),
              nrm[3], tm=min(512, s), tf=512)
    return xs.reshape(b, s, d)
```

```python
import functools

import jax
import jax.numpy as jnp
from jax import lax
from jax.experimental import pallas as pl
from jax.experimental.pallas import tpu as pltpu

HEAD_DIM = 128
N_MEM = 256
MEM_HEADS = 4
DIL_GROUPS = ((128, 1), (512, 4), (2048, 16))
HEADS_PER_GROUP = 4
DIL_HEADS = len(DIL_GROUPS) * HEADS_PER_GROUP
SB_HEADS = 12
BLOCK = 128
CONV_WIDTH = 3
ROPE_THETA = 10000.0
EPS = 1e-6
NEG_INF = -1e30
EXP_UNDERFLOW_F32 = -105.0

DIL_W = DIL_HEADS * HEAD_DIM
MEM_W = MEM_HEADS * HEAD_DIM
SB_W = SB_HEADS * HEAD_DIM
GROUP_W = HEADS_PER_GROUP * HEAD_DIM
DIL_ROWS = BLOCK * max(d for _, d in DIL_GROUPS)

F32 = jnp.float32
BF16 = jnp.bfloat16

VMEM_LIMIT_BYTES = 56 * 1024 * 1024
SUBLANE_BF16 = 16


def _params(*sem):
    return pltpu.CompilerParams(dimension_semantics=sem,
                                vmem_limit_bytes=VMEM_LIMIT_BYTES)


def _rms(xf, g):
    y = xf * lax.rsqrt(jnp.mean(xf * xf, axis=-1, keepdims=True) + EPS)
    return y * g


def _dot(a, b):
    return jnp.dot(a, b, preferred_element_type=F32)


def _dot_nt(a, b):
    return lax.dot_general(a, b, (((1,), (1,)), ((), ())),
                           preferred_element_type=F32)


def _norm_matmul_kernel(*refs, rope_tiles, heads_per_tile, gain_split):
    if rope_tiles:
        x_ref, g_ref, w_ref, cos_ref, sin_ref, o_ref, h_ref = refs
    else:
        x_ref, g_ref, w_ref, o_ref, h_ref = refs
    j = pl.program_id(1)
    n_gains = h_ref.shape[0]

    @pl.when(j == 0)
    def _():
        xf = x_ref[...]
        y = xf * lax.rsqrt(jnp.mean(xf * xf, axis=-1, keepdims=True) + EPS)
        for gi in range(n_gains):
            h_ref[gi] = (y * g_ref[gi:gi + 1, :]).astype(BF16)

    if n_gains == 1:
        lhs = h_ref[0]
    else:
        lhs = h_ref[jnp.where(j >= gain_split, 1, 0)]
    acc = _dot(lhs, w_ref[...])

    if rope_tiles:
        is_rope = j < rope_tiles
        c = cos_ref[...]
        s = sin_ref[...]
        for hh in range(heads_per_tile):
            sl = slice(hh * HEAD_DIM, (hh + 1) * HEAD_DIM)
            seg = acc[:, sl]
            rot = pltpu.roll(seg, HEAD_DIM // 2, 1)
            o_ref[:, sl] = jnp.where(is_rope, seg * c + rot * s,
                                     seg).astype(o_ref.dtype)
    else:
        o_ref[...] = acc.astype(o_ref.dtype)


def _norm_matmul(x, gains, w, *, tm, tn, out_dtype, gain_split_col=0,
                 rope_cols=0, cos2=None, sin2=None):
    m, k = x.shape
    n = w.shape[1]
    n_gains = gains.shape[0]
    assert m % tm == 0 and n % tn == 0
    assert rope_cols % tn == 0 and gain_split_col % tn == 0
    in_specs = [
        pl.BlockSpec((tm, k), lambda i, j: (i, 0)),
        pl.BlockSpec((n_gains, k), lambda i, j: (0, 0)),
        pl.BlockSpec((k, tn), lambda i, j: (0, j)),
    ]
    args = [x, gains, w]
    if rope_cols:
        in_specs += [pl.BlockSpec((tm, HEAD_DIM), lambda i, j: (i, 0))] * 2
        args += [cos2, sin2]
    return pl.pallas_call(
        functools.partial(_norm_matmul_kernel, rope_tiles=rope_cols // tn,
                          heads_per_tile=tn // HEAD_DIM,
                          gain_split=gain_split_col // tn),
        grid=(m // tm, n // tn),
        in_specs=in_specs,
        out_specs=pl.BlockSpec((tm, tn), lambda i, j: (i, j)),
        out_shape=jax.ShapeDtypeStruct((m, n), out_dtype),
        scratch_shapes=[pltpu.VMEM((n_gains, tm, k), BF16)],
        compiler_params=_params("parallel", "arbitrary"),
        name="norm_matmul",
    )(*args)


def _dilated_kernel(q_ref, kc_ref, vc_ref, kp_ref, vp_ref, o_ref, l_ref, *,
                    dilation, chains_per_stage):
    n = pl.program_id(0)
    scale = HEAD_DIM ** -0.5
    nq = DIL_ROWS // (BLOCK * dilation)
    qi = lax.broadcasted_iota(jnp.int32, (BLOCK, BLOCK), 0)
    kj = lax.broadcasted_iota(jnp.int32, (BLOCK, BLOCK), 1)
    band = kj >= qi
    valid_cur = kj <= qi

    def rows(ref, b, r):
        start = b * BLOCK * dilation + r
        if dilation == 1:
            return pl.ds(start, BLOCK)
        return pl.ds(start, BLOCK, stride=dilation)

    loaded = {}

    def kv(b, r):
        if (b, r) not in loaded:
            if b < 0:
                idx = rows(kp_ref, 0, r)
                loaded[(b, r)] = (kp_ref[idx, :].astype(BF16),
                                  vp_ref[idx, :].astype(BF16))
            else:
                idx = rows(kc_ref, b, r)
                loaded[(b, r)] = (kc_ref[idx, :].astype(BF16),
                                  vc_ref[idx, :].astype(BF16))
        return loaded[(b, r)]

    def run_stage(chains):
        qs, kps, kcs, vps, vcs, masks, idxs = [], [], [], [], [], [], []
        for b, r in chains:
            idx = rows(q_ref, b, r)
            qs.append(q_ref[idx, :].astype(BF16))
            kp, vp = kv(b - 1, r)
            kc, vc = kv(b, r)
            kps.append(kp); vps.append(vp); kcs.append(kc); vcs.append(vc)
            masks.append(band if b > 0 else band & (n > 0))
            idxs.append(idx)
        sp = [jnp.where(m, _dot_nt(q, k) * scale, NEG_INF)
              for q, k, m in zip(qs, kps, masks)]
        sc = [jnp.where(valid_cur, _dot_nt(q, k) * scale, NEG_INF)
              for q, k in zip(qs, kcs)]
        mx = [jnp.maximum(jnp.max(a, axis=-1, keepdims=True),
                          jnp.max(c, axis=-1, keepdims=True))
              for a, c in zip(sp, sc)]
        tot = [jnp.sum(jnp.exp(a - m), axis=-1, keepdims=True)
               + jnp.sum(jnp.exp(c - m), axis=-1, keepdims=True)
               for a, c, m in zip(sp, sc, mx)]
        lse = [m + jnp.log(t) for m, t in zip(mx, tot)]
        pp = [jnp.exp(a - l).astype(BF16) for a, l in zip(sp, lse)]
        pc = [jnp.exp(c - l).astype(BF16) for c, l in zip(sc, lse)]
        for idx, a, c, vp, vc, l in zip(idxs, pp, pc, vps, vcs, lse):
            o_ref[idx, :] = _dot(a, vp) + _dot(c, vc)
            l_ref[idx, :] = jnp.broadcast_to(l, (BLOCK, HEAD_DIM))

    chains = [(b, r) for b in range(nq) for r in range(dilation)]
    for c0 in range(0, len(chains), chains_per_stage):
        run_stage(chains[c0:c0 + chains_per_stage])


def _dilated_group(proj, gi, dilation):
    s = proj.shape[0]
    assert s % DIL_ROWS == 0 and DIL_ROWS % (BLOCK * dilation) == 0
    prev_rows = BLOCK * dilation
    ratio = DIL_ROWS // prev_rows
    q_col = gi * HEADS_PER_GROUP
    k_col = DIL_HEADS + q_col
    v_col = 2 * DIL_HEADS + q_col

    def cur(col):
        return pl.BlockSpec((DIL_ROWS, HEAD_DIM), lambda n, h: (n, col + h))

    def prev(col):
        return pl.BlockSpec(
            (prev_rows, HEAD_DIM),
            lambda n, h: (jnp.maximum(n * ratio - 1, 0), col + h))

    out_spec = pl.BlockSpec((DIL_ROWS, HEAD_DIM), lambda n, h: (n, h))
    return pl.pallas_call(
        functools.partial(_dilated_kernel, dilation=dilation,
                          chains_per_stage=4),
        grid=(s // DIL_ROWS, HEADS_PER_GROUP),
        in_specs=[cur(q_col), cur(k_col), cur(v_col), prev(k_col),
                  prev(v_col)],
        out_specs=[out_spec, out_spec],
        out_shape=[jax.ShapeDtypeStruct((s, GROUP_W), F32)] * 2,
        compiler_params=_params("parallel", "arbitrary"),
        name=f"dilated_d{dilation}",
    )(proj, proj, proj, proj, proj)


def _mem_attn_kernel(q_ref, mkv_ref, o_ref):
    scale = HEAD_DIM ** -0.5
    for hh in range(MEM_HEADS):
        sl = slice(hh * HEAD_DIM, (hh + 1) * HEAD_DIM)
        vl = slice(MEM_W + hh * HEAD_DIM, MEM_W + (hh + 1) * HEAD_DIM)
        s = _dot_nt(q_ref[:, sl].astype(BF16), mkv_ref[:, sl]) * scale
        e = jnp.exp(s - jnp.max(s, axis=-1, keepdims=True))
        p = e / jnp.sum(e, axis=-1, keepdims=True)
        o_ref[:, sl] = _dot(p.astype(BF16), mkv_ref[:, vl]).astype(o_ref.dtype)


def _mem_attn(proj, col_tile, mkv, *, tm):
    s = proj.shape[0]
    n_mem = mkv.shape[0]
    return pl.pallas_call(
        _mem_attn_kernel,
        grid=(s // tm,),
        in_specs=[pl.BlockSpec((tm, MEM_W), lambda i: (i, col_tile)),
                  pl.BlockSpec((n_mem, 2 * MEM_W), lambda i: (0, 0))],
        out_specs=pl.BlockSpec((tm, MEM_W), lambda i: (i, 0)),
        out_shape=jax.ShapeDtypeStruct((s, MEM_W), BF16),
        compiler_params=_params("parallel"),
        name="mem_attn",
    )(proj, mkv)


def _outproj_merge_kernel(o1, o2, o3, l1, l2, l3, om_ref, x_ref, w_ref, g_ref,
                          out_ref):
    a1, a2, a3 = l1[...], l2[...], l3[...]
    mx = jnp.maximum(jnp.maximum(a1, a2), a3)
    e1, e2, e3 = jnp.exp(a1 - mx), jnp.exp(a2 - mx), jnp.exp(a3 - mx)
    den = e1 + e2 + e3
    o_dil = (e1 / den) * o1[...] + (e2 / den) * o2[...] + (e3 / den) * o3[...]
    heads = jnp.concatenate([o_dil.astype(BF16), om_ref[...]], axis=1)
    y = _dot(heads, w_ref[...])
    out_ref[...] = x_ref[...] + _rms(y, g_ref[...])


def _outproj_merge(outs, lses, o_mem, x, w_o, g, *, tm):
    s, d = x.shape
    row = lambda w: pl.BlockSpec((tm, w), lambda i: (i, 0))
    full = lambda a: pl.BlockSpec(a.shape, lambda i: (0, 0))
    g2 = g.reshape(1, d)
    return pl.pallas_call(
        _outproj_merge_kernel,
        grid=(s // tm,),
        in_specs=[row(GROUP_W)] * 6 + [row(MEM_W), row(d), full(w_o), full(g2)],
        out_specs=row(d),
        out_shape=jax.ShapeDtypeStruct((s, d), F32),
        compiler_params=_params("parallel"),
        name="outproj_merge",
    )(*outs, *lses, o_mem, x, w_o, g2)


def _outproj_kernel(oa_ref, om_ref, x_ref, w_ref, g_ref, out_ref):
    heads = jnp.concatenate([oa_ref[...], om_ref[...]], axis=1)
    y = _dot(heads, w_ref[...])
    out_ref[...] = x_ref[...] + _rms(y, g_ref[...])


def _outproj(o_attn, o_mem, x, w_o, g, *, tm):
    s, d = x.shape
    row = lambda w: pl.BlockSpec((tm, w), lambda i: (i, 0))
    full = lambda a: pl.BlockSpec(a.shape, lambda i: (0, 0))
    g2 = g.reshape(1, d)
    return pl.pallas_call(
        _outproj_kernel,
        grid=(s // tm,),
        in_specs=[row(o_attn.shape[1]), row(MEM_W), row(d), full(w_o),
                  full(g2)],
        out_specs=row(d),
        out_shape=jax.ShapeDtypeStruct((s, d), F32),
        compiler_params=_params("parallel"),
        name="outproj",
    )(o_attn, o_mem, x, w_o, g2)


def _ffn_kernel(x_ref, xh_ref, gin_ref, wg_ref, wv_ref, wc_ref, bc_ref, wd_ref,
                gout_ref, o_ref, hn_ref, hh_ref, gbuf_ref, *, tm):
    i = pl.program_id(0)
    j = pl.program_id(1)
    halo = SUBLANE_BF16

    @pl.when(j == 0)
    def _():
        hn_ref[...] = _rms(x_ref[...], gin_ref[...]).astype(BF16)
        hh_ref[...] = _rms(xh_ref[...], gin_ref[...]).astype(BF16)
        o_ref[...] = jnp.zeros_like(o_ref)

    wg = wg_ref[...]
    gate = _dot(hn_ref[...], wg)
    val = _dot(hn_ref[...], wv_ref[...])
    gate_halo = jnp.where(i > 0, _dot(hh_ref[...], wg), 0.0)
    gbuf_ref[0:halo, :] = gate_halo
    gbuf_ref[halo:halo + tm, :] = gate
    g_m1 = gbuf_ref[pl.ds(halo - 1, tm), :]
    g_m2 = gbuf_ref[pl.ds(halo - 2, tm), :]
    acc = bc_ref[...] + gate * wc_ref[2:3, :]
    acc = acc + g_m2 * wc_ref[0:1, :]
    acc = acc + g_m1 * wc_ref[1:2, :]
    act = (acc * (1.0 / (1.0 + jnp.exp(-acc)))) * val
    o_ref[...] += _dot(act.astype(BF16), wd_ref[...])

    @pl.when(j == pl.num_programs(1) - 1)
    def _():
        o_ref[...] = x_ref[...] + _rms(o_ref[...], gout_ref[...])


def _ffn(x, g_in, w_up, w_conv, b_conv, w_down, g_out, *, tm, tf):
    s, d = x.shape
    d_ff = w_down.shape[0]
    assert s % tm == 0 and d_ff % tf == 0 and tm % SUBLANE_BF16 == 0
    nf = d_ff // tf
    halo = SUBLANE_BF16
    return pl.pallas_call(
        functools.partial(_ffn_kernel, tm=tm),
        grid=(s // tm, nf),
        in_specs=[
            pl.BlockSpec((tm, d), lambda i, j: (i, 0)),
            pl.BlockSpec((halo, d),
                         lambda i, j: (jnp.maximum(i * (tm // halo) - 1, 0), 0)),
            pl.BlockSpec((1, d), lambda i, j: (0, 0)),
            pl.BlockSpec((d, tf), lambda i, j: (0, j)),
            pl.BlockSpec((d, tf), lambda i, j: (0, nf + j)),
            pl.BlockSpec((CONV_WIDTH, tf), lambda i, j: (0, j)),
            pl.BlockSpec((1, tf), lambda i, j: (0, j)),
            pl.BlockSpec((tf, d), lambda i, j: (j, 0)),
            pl.BlockSpec((1, d), lambda i, j: (0, 0)),
        ],
        out_specs=pl.BlockSpec((tm, d), lambda i, j: (i, 0)),
        out_shape=jax.ShapeDtypeStruct((s, d), F32),
        scratch_shapes=[pltpu.VMEM((tm, d), BF16),
                        pltpu.VMEM((halo, d), BF16),
                        pltpu.VMEM((halo + tm, tf), F32)],
        compiler_params=_params("parallel", "arbitrary"),
        name="conv_ffn",
    )(x, x, g_in.reshape(1, d), w_up, w_up, w_conv, b_conv.reshape(1, d_ff),
      w_down, g_out.reshape(1, d))


def _sb_kernel(q_ref, k_ref, v_ref, o_ref, acc_ref, carry_ref, *, tq):
    i = pl.program_id(1)
    scale = HEAD_DIM ** -0.5
    half = tq // 2
    q = q_ref[...]

    rr = lax.broadcasted_iota(jnp.int32, (2 * half, 2 * half), 0)
    cc = lax.broadcasted_iota(jnp.int32, (2 * half, 2 * half), 1)
    tri = jnp.where((cc >= half) | ((rr & (half - 1)) > cc), 1.0, 0.0).astype(BF16)

    def tile(kstart, nk, causal):
        kb = k_ref[pl.ds(kstart, nk), :]
        vb = v_ref[pl.ds(kstart, nk), :]
        z = _dot_nt(q, kb) * scale
        soft = jnp.log1p(jnp.exp(-jnp.abs(z)))
        log_sig = jnp.minimum(z, 0.0) - soft
        log_keep = -jnp.maximum(z, 0.0) - soft
        if causal is not None:
            log_keep = jnp.where(causal, log_keep, 0.0)
        hi = log_keep.astype(BF16)
        lo = (log_keep - hi.astype(F32)).astype(BF16)
        carry = carry_ref[...]
        parts = [None] * (nk // half)
        for c in reversed(range(nk // half)):
            sl = slice(c * half, (c + 1) * half)
            res = _dot(jnp.concatenate([hi[:, sl], lo[:, sl]], axis=1), tri)
            parts[c] = jnp.exp(log_sig[:, sl] + (carry + res[:, :half]))
            carry = carry + res[:, half:]
        a = jnp.concatenate(parts, axis=1)
        if causal is not None:
            a = jnp.where(causal, a, 0.0)
        acc_ref[...] += _dot(a.astype(BF16), vb)
        carry_ref[...] = carry

    acc_ref[...] = jnp.zeros_like(acc_ref)
    carry_ref[...] = jnp.zeros_like(carry_ref)

    @pl.when(i == 0)
    def _():
        qi = lax.broadcasted_iota(jnp.int32, (tq, tq), 0)
        kj = lax.broadcasted_iota(jnp.int32, (tq, tq), 1)
        tile(0, tq, kj < qi)

    @pl.when(i > 0)
    def _():
        qi = lax.broadcasted_iota(jnp.int32, (tq, 2 * tq), 0)
        kj = lax.broadcasted_iota(jnp.int32, (tq, 2 * tq), 1)
        tile(pl.multiple_of((i - 1) * tq, tq), 2 * tq, kj < qi + tq)

    def live():
        return jnp.max(carry_ref[...]) > EXP_UNDERFLOW_F32

    def cond(c):
        step, alive = c
        return (step < i) & alive

    def body(c):
        step, _ = c
        tile(pl.multiple_of((i - 1 - step) * tq, tq), tq, None)
        return step + 1, live()

    lax.while_loop(cond, body, (jnp.int32(1), live()))
    o_ref[...] = acc_ref[...].astype(o_ref.dtype)


def _stick_breaking(kvq, *, tq):
    s = kvq.shape[0]
    assert tq == 2 * HEAD_DIM and s % tq == 0
    return pl.pallas_call(
        functools.partial(_sb_kernel, tq=tq),
        grid=(SB_HEADS, s // tq),
        in_specs=[pl.BlockSpec((tq, HEAD_DIM),
                               lambda h, i: (i, 2 * SB_HEADS + h)),
                  pl.BlockSpec((s, HEAD_DIM), lambda h, i: (0, h)),
                  pl.BlockSpec((s, HEAD_DIM), lambda h, i: (0, SB_HEADS + h))],
        out_specs=pl.BlockSpec((tq, HEAD_DIM), lambda h, i: (i, h)),
        out_shape=jax.ShapeDtypeStruct((s, SB_W), BF16),
        scratch_shapes=[pltpu.VMEM((tq, HEAD_DIM), F32),
                        pltpu.VMEM((tq, HEAD_DIM), F32)],
        compiler_params=_params("parallel", "arbitrary"),
        name="stick_breaking",
    )(kvq, kvq, kvq)


def _rope_tables(s):
    pos = jnp.arange(s, dtype=F32)
    inv = ROPE_THETA ** (-jnp.arange(0, HEAD_DIM, 2, dtype=F32) / HEAD_DIM)
    ang = pos[:, None] * inv[None, :]
    cos, sin = jnp.cos(ang), jnp.sin(ang)
    return (jnp.concatenate([cos, cos], axis=-1),
            jnp.concatenate([-sin, sin], axis=-1))


def kernel(x, mem, norms, w_in_a, w_o_a, g_kv, w_kv, w_in_b, w_o_b, w_mem_kv,
           w_up, w_conv, b_conv, w_down):
    b, s, d = x.shape
    assert b == 1 and norms.shape[0] == 2
    xs = x.reshape(s, d)
    mems = mem.reshape(mem.shape[1], d)
    bf = lambda a: a.astype(BF16)
    tm = min(1024, s)

    nrm = norms[0]
    cos2, sin2 = _rope_tables(s)
    proj = _norm_matmul(xs, nrm[0:1], bf(w_in_a[0]), tm=tm, tn=GROUP_W,
                        out_dtype=F32, rope_cols=2 * DIL_W, cos2=cos2,
                        sin2=sin2)
    mkv = _norm_matmul(mems, nrm[4:5], bf(w_mem_kv[0]), tm=mems.shape[0],
                       tn=GROUP_W, out_dtype=BF16)
    outs, lses = [], []
    for gi, (_, dilation) in enumerate(DIL_GROUPS):
        o_g, l_g = _dilated_group(proj, gi, dilation)
        outs.append(o_g)
        lses.append(l_g)
    o_mem = _mem_attn(proj, 3 * DIL_W // MEM_W, mkv, tm=min(512, s))
    xs = _outproj_merge(outs, lses, o_mem, xs, bf(w_o_a[0]), nrm[1],
                        tm=min(512, s))
    xs = _ffn(xs, nrm[2], bf(w_up[0]), w_conv[0], b_conv[0], bf(w_down[0]),
              nrm[3], tm=min(512, s), tf=512)

    nrm = norms[1]
    kvq = _norm_matmul(xs, jnp.stack([g_kv, nrm[0]]),
                       bf(jnp.concatenate([w_kv, w_in_b[0]], axis=1)),
                       tm=tm, tn=GROUP_W, out_dtype=BF16,
                       gain_split_col=2 * SB_W)
    mkv = _norm_matmul(mems, nrm[4:5], bf(w_mem_kv[1]), tm=mems.shape[0],
                       tn=GROUP_W, out_dtype=BF16)
    o_sb = _stick_breaking(kvq, tq=2 * HEAD_DIM)
    o_mem = _mem_attn(kvq, 3 * SB_W // MEM_W, mkv, tm=min(512, s))
    xs = _outproj(o_sb, o_mem, xs, bf(w_o_b[0]), nrm[1], tm=min(512, s))
    xs = _ffn(xs, nrm[2], bf(w_up[1]), w_conv[1], b_conv[1], bf(w_down[1]),
              nrm[3], tm=min(512, s), tf=512)
    return xs.reshape(b, s, d)
```

```python
import functools

import jax
import jax.numpy as jnp
from jax import lax
from jax.experimental import pallas as pl
from jax.experimental.pallas import tpu as pltpu

HEAD_DIM = 128
N_MEM = 256
MEM_HEADS = 4
DIL_GROUPS = ((128, 1), (512, 4), (2048, 16))
HEADS_PER_GROUP = 4
DIL_HEADS = len(DIL_GROUPS) * HEADS_PER_GROUP
SB_HEADS = 12
BLOCK = 128
CONV_WIDTH = 3
ROPE_THETA = 10000.0
EPS = 1e-6
NEG_INF = -1e30
EXP_UNDERFLOW_F32 = -105.0
NEG_LOG2E = -1.4426950408889634

DIL_W = DIL_HEADS * HEAD_DIM
MEM_W = MEM_HEADS * HEAD_DIM
SB_W = SB_HEADS * HEAD_DIM
GROUP_W = HEADS_PER_GROUP * HEAD_DIM
DIL_ROWS = BLOCK * max(d for _, d in DIL_GROUPS)

F32 = jnp.float32
BF16 = jnp.bfloat16

VMEM_LIMIT_BYTES = 56 * 1024 * 1024
SUBLANE_BF16 = 16


def _params(*sem):
    return pltpu.CompilerParams(dimension_semantics=sem,
                                vmem_limit_bytes=VMEM_LIMIT_BYTES)


def _rms(xf, g):
    y = xf * lax.rsqrt(jnp.mean(xf * xf, axis=-1, keepdims=True) + EPS)
    return y * g


def _dot(a, b):
    return jnp.dot(a, b, preferred_element_type=F32)


def _dot_nt(a, b):
    return lax.dot_general(a, b, (((1,), (1,)), ((), ())),
                           preferred_element_type=F32)


def _norm_matmul_kernel(*refs, rope_tiles, heads_per_tile, gain_split):
    if rope_tiles:
        x_ref, g_ref, w_ref, cos_ref, sin_ref, o_ref, h_ref = refs
    else:
        x_ref, g_ref, w_ref, o_ref, h_ref = refs
    j = pl.program_id(1)
    n_gains = h_ref.shape[0]

    @pl.when(j == 0)
    def _():
        xf = x_ref[...]
        y = xf * lax.rsqrt(jnp.mean(xf * xf, axis=-1, keepdims=True) + EPS)
        for gi in range(n_gains):
            h_ref[gi] = (y * g_ref[gi:gi + 1, :]).astype(BF16)

    if n_gains == 1:
        lhs = h_ref[0]
    else:
        lhs = h_ref[jnp.where(j >= gain_split, 1, 0)]
    acc = _dot(lhs, w_ref[...])

    if rope_tiles:
        is_rope = j < rope_tiles
        c = cos_ref[...]
        s = sin_ref[...]
        for hh in range(heads_per_tile):
            sl = slice(hh * HEAD_DIM, (hh + 1) * HEAD_DIM)
            seg = acc[:, sl]
            rot = pltpu.roll(seg, HEAD_DIM // 2, 1)
            o_ref[:, sl] = jnp.where(is_rope, seg * c + rot * s,
                                     seg).astype(o_ref.dtype)
    else:
        o_ref[...] = acc.astype(o_ref.dtype)


def _norm_matmul(x, gains, w, *, tm, tn, out_dtype, gain_split_col=0,
                 rope_cols=0, cos2=None, sin2=None):
    m, k = x.shape
    n = w.shape[1]
    n_gains = gains.shape[0]
    assert m % tm == 0 and n % tn == 0
    assert rope_cols % tn == 0 and gain_split_col % tn == 0
    in_specs = [
        pl.BlockSpec((tm, k), lambda i, j: (i, 0)),
        pl.BlockSpec((n_gains, k), lambda i, j: (0, 0)),
        pl.BlockSpec((k, tn), lambda i, j: (0, j)),
    ]
    args = [x, gains, w]
    if rope_cols:
        in_specs += [pl.BlockSpec((tm, HEAD_DIM), lambda i, j: (i, 0))] * 2
        args += [cos2, sin2]
    return pl.pallas_call(
        functools.partial(_norm_matmul_kernel, rope_tiles=rope_cols // tn,
                          heads_per_tile=tn // HEAD_DIM,
                          gain_split=gain_split_col // tn),
        grid=(m // tm, n // tn),
        in_specs=in_specs,
        out_specs=pl.BlockSpec((tm, tn), lambda i, j: (i, j)),
        out_shape=jax.ShapeDtypeStruct((m, n), out_dtype),
        scratch_shapes=[pltpu.VMEM((n_gains, tm, k), BF16)],
        compiler_params=_params("parallel", "arbitrary"),
        name="norm_matmul",
    )(*args)


def _dilated_kernel(q_ref, kc_ref, vc_ref, kp_ref, vp_ref, o_ref, l_ref, *,
                    dilation, chains_per_stage):
    n = pl.program_id(0)
    scale = HEAD_DIM ** -0.5
    nq = DIL_ROWS // (BLOCK * dilation)
    qi = lax.broadcasted_iota(jnp.int32, (BLOCK, BLOCK), 0)
    kj = lax.broadcasted_iota(jnp.int32, (BLOCK, BLOCK), 1)
    band = kj >= qi
    valid_cur = kj <= qi

    def rows(ref, b, r):
        start = b * BLOCK * dilation + r
        if dilation == 1:
            return pl.ds(start, BLOCK)
        return pl.ds(start, BLOCK, stride=dilation)

    loaded = {}

    def kv(b, r):
        if (b, r) not in loaded:
            if b < 0:
                idx = rows(kp_ref, 0, r)
                loaded[(b, r)] = (kp_ref[idx, :].astype(BF16),
                                  vp_ref[idx, :].astype(BF16))
            else:
                idx = rows(kc_ref, b, r)
                loaded[(b, r)] = (kc_ref[idx, :].astype(BF16),
                                  vc_ref[idx, :].astype(BF16))
        return loaded[(b, r)]

    def run_stage(chains):
        qs, kps, kcs, vps, vcs, masks, idxs = [], [], [], [], [], [], []
        for b, r in chains:
            idx = rows(q_ref, b, r)
            qs.append(q_ref[idx, :].astype(BF16))
            kp, vp = kv(b - 1, r)
            kc, vc = kv(b, r)
            kps.append(kp); vps.append(vp); kcs.append(kc); vcs.append(vc)
            masks.append(band if b > 0 else band & (n > 0))
            idxs.append(idx)
        sp = [jnp.where(m, _dot_nt(q, k) * scale, NEG_INF)
              for q, k, m in zip(qs, kps, masks)]
        sc = [jnp.where(valid_cur, _dot_nt(q, k) * scale, NEG_INF)
              for q, k in zip(qs, kcs)]
        mx = [jnp.maximum(jnp.max(a, axis=-1, keepdims=True),
                          jnp.max(c, axis=-1, keepdims=True))
              for a, c in zip(sp, sc)]
        tot = [jnp.sum(jnp.exp(a - m), axis=-1, keepdims=True)
               + jnp.sum(jnp.exp(c - m), axis=-1, keepdims=True)
               for a, c, m in zip(sp, sc, mx)]
        lse = [m + jnp.log(t) for m, t in zip(mx, tot)]
        pp = [jnp.exp(a - l).astype(BF16) for a, l in zip(sp, lse)]
        pc = [jnp.exp(c - l).astype(BF16) for c, l in zip(sc, lse)]
        for idx, a, c, vp, vc, l in zip(idxs, pp, pc, vps, vcs, lse):
            o_ref[idx, :] = _dot(a, vp) + _dot(c, vc)
            l_ref[idx, :] = jnp.broadcast_to(l, (BLOCK, HEAD_DIM))

    chains = [(b, r) for b in range(nq) for r in range(dilation)]
    for c0 in range(0, len(chains), chains_per_stage):
        run_stage(chains[c0:c0 + chains_per_stage])


def _dilated_group(proj, gi, dilation):
    s = proj.shape[0]
    assert s % DIL_ROWS == 0 and DIL_ROWS % (BLOCK * dilation) == 0
    prev_rows = BLOCK * dilation
    ratio = DIL_ROWS // prev_rows
    q_col = gi * HEADS_PER_GROUP
    k_col = DIL_HEADS + q_col
    v_col = 2 * DIL_HEADS + q_col

    def cur(col):
        return pl.BlockSpec((DIL_ROWS, HEAD_DIM), lambda n, h: (n, col + h))

    def prev(col):
        return pl.BlockSpec(
            (prev_rows, HEAD_DIM),
            lambda n, h: (jnp.maximum(n * ratio - 1, 0), col + h))

    out_spec = pl.BlockSpec((DIL_ROWS, HEAD_DIM), lambda n, h: (n, h))
    return pl.pallas_call(
        functools.partial(_dilated_kernel, dilation=dilation,
                          chains_per_stage=4),
        grid=(s // DIL_ROWS, HEADS_PER_GROUP),
        in_specs=[cur(q_col), cur(k_col), cur(v_col), prev(k_col),
                  prev(v_col)],
        out_specs=[out_spec, out_spec],
        out_shape=[jax.ShapeDtypeStruct((s, GROUP_W), F32)] * 2,
        compiler_params=_params("parallel", "arbitrary"),
        name=f"dilated_d{dilation}",
    )(proj, proj, proj, proj, proj)


def _mem_attn_kernel(q_ref, mkv_ref, o_ref):
    scale = HEAD_DIM ** -0.5
    for hh in range(MEM_HEADS):
        sl = slice(hh * HEAD_DIM, (hh + 1) * HEAD_DIM)
        vl = slice(MEM_W + hh * HEAD_DIM, MEM_W + (hh + 1) * HEAD_DIM)
        s = _dot_nt(q_ref[:, sl].astype(BF16), mkv_ref[:, sl]) * scale
        e = jnp.exp(s - jnp.max(s, axis=-1, keepdims=True))
        p = e / jnp.sum(e, axis=-1, keepdims=True)
        o_ref[:, sl] = _dot(p.astype(BF16), mkv_ref[:, vl]).astype(o_ref.dtype)


def _mem_attn(proj, col_tile, mkv, *, tm):
    s = proj.shape[0]
    n_mem = mkv.shape[0]
    return pl.pallas_call(
        _mem_attn_kernel,
        grid=(s // tm,),
        in_specs=[pl.BlockSpec((tm, MEM_W), lambda i: (i, col_tile)),
                  pl.BlockSpec((n_mem, 2 * MEM_W), lambda i: (0, 0))],
        out_specs=pl.BlockSpec((tm, MEM_W), lambda i: (i, 0)),
        out_shape=jax.ShapeDtypeStruct((s, MEM_W), BF16),
        compiler_params=_params("parallel"),
        name="mem_attn",
    )(proj, mkv)


def _outproj_merge_kernel(o1, o2, o3, l1, l2, l3, om_ref, x_ref, w_ref, g_ref,
                          out_ref):
    a1, a2, a3 = l1[...], l2[...], l3[...]
    mx = jnp.maximum(jnp.maximum(a1, a2), a3)
    e1, e2, e3 = jnp.exp(a1 - mx), jnp.exp(a2 - mx), jnp.exp(a3 - mx)
    den = e1 + e2 + e3
    o_dil = (e1 / den) * o1[...] + (e2 / den) * o2[...] + (e3 / den) * o3[...]
    heads = jnp.concatenate([o_dil.astype(BF16), om_ref[...]], axis=1)
    y = _dot(heads, w_ref[...])
    out_ref[...] = x_ref[...] + _rms(y, g_ref[...])


def _outproj_merge(outs, lses, o_mem, x, w_o, g, *, tm):
    s, d = x.shape
    row = lambda w: pl.BlockSpec((tm, w), lambda i: (i, 0))
    full = lambda a: pl.BlockSpec(a.shape, lambda i: (0, 0))
    g2 = g.reshape(1, d)
    return pl.pallas_call(
        _outproj_merge_kernel,
        grid=(s // tm,),
        in_specs=[row(GROUP_W)] * 6 + [row(MEM_W), row(d), full(w_o), full(g2)],
        out_specs=row(d),
        out_shape=jax.ShapeDtypeStruct((s, d), F32),
        compiler_params=_params("parallel"),
        name="outproj_merge",
    )(*outs, *lses, o_mem, x, w_o, g2)


def _outproj_kernel(oa_ref, om_ref, x_ref, w_ref, g_ref, out_ref):
    heads = jnp.concatenate([oa_ref[...], om_ref[...]], axis=1)
    y = _dot(heads, w_ref[...])
    out_ref[...] = x_ref[...] + _rms(y, g_ref[...])


def _outproj(o_attn, o_mem, x, w_o, g, *, tm):
    s, d = x.shape
    row = lambda w: pl.BlockSpec((tm, w), lambda i: (i, 0))
    full = lambda a: pl.BlockSpec(a.shape, lambda i: (0, 0))
    g2 = g.reshape(1, d)
    return pl.pallas_call(
        _outproj_kernel,
        grid=(s // tm,),
        in_specs=[row(o_attn.shape[1]), row(MEM_W), row(d), full(w_o),
                  full(g2)],
        out_specs=row(d),
        out_shape=jax.ShapeDtypeStruct((s, d), F32),
        compiler_params=_params("parallel"),
        name="outproj",
    )(o_attn, o_mem, x, w_o, g2)


def _ffn_kernel(x_ref, xh_ref, gin_ref, wg_ref, wv_ref, wc_ref, bc_ref, wd_ref,
                gout_ref, o_ref, hn_ref, hh_ref, gbuf_ref, *, tm):
    i = pl.program_id(0)
    j = pl.program_id(1)
    halo = SUBLANE_BF16

    @pl.when(j == 0)
    def _():
        hn_ref[...] = _rms(x_ref[...], gin_ref[...]).astype(BF16)
        hh_ref[...] = _rms(xh_ref[...], gin_ref[...]).astype(BF16)
        o_ref[...] = jnp.zeros_like(o_ref)

    wg = wg_ref[...]
    gate = _dot(hn_ref[...], wg)
    val = _dot(hn_ref[...], wv_ref[...])
    gate_halo = jnp.where(i > 0, _dot(hh_ref[...], wg), 0.0)
    gbuf_ref[0:halo, :] = gate_halo
    gbuf_ref[halo:halo + tm, :] = gate
    g_m1 = gbuf_ref[pl.ds(halo - 1, tm), :]
    g_m2 = gbuf_ref[pl.ds(halo - 2, tm), :]
    acc = bc_ref[...] + gate * wc_ref[2:3, :]
    acc = acc + g_m2 * wc_ref[0:1, :]
    acc = acc + g_m1 * wc_ref[1:2, :]
    act = (acc * (1.0 / (1.0 + jnp.exp(-acc)))) * val
    o_ref[...] += _dot(act.astype(BF16), wd_ref[...])

    @pl.when(j == pl.num_programs(1) - 1)
    def _():
        o_ref[...] = x_ref[...] + _rms(o_ref[...], gout_ref[...])


def _ffn(x, g_in, w_up, w_conv, b_conv, w_down, g_out, *, tm, tf):
    s, d = x.shape
    d_ff = w_down.shape[0]
    assert s % tm == 0 and d_ff % tf == 0 and tm % SUBLANE_BF16 == 0
    nf = d_ff // tf
    halo = SUBLANE_BF16
    return pl.pallas_call(
        functools.partial(_ffn_kernel, tm=tm),
        grid=(s // tm, nf),
        in_specs=[
            pl.BlockSpec((tm, d), lambda i, j: (i, 0),
                         pipeline_mode=pl.Buffered(1)),
            pl.BlockSpec((halo, d),
                         lambda i, j: (jnp.maximum(i * (tm // halo) - 1, 0), 0)),
            pl.BlockSpec((1, d), lambda i, j: (0, 0)),
            pl.BlockSpec((d, tf), lambda i, j: (0, j)),
            pl.BlockSpec((d, tf), lambda i, j: (0, nf + j)),
            pl.BlockSpec((CONV_WIDTH, tf), lambda i, j: (0, j)),
            pl.BlockSpec((1, tf), lambda i, j: (0, j)),
            pl.BlockSpec((tf, d), lambda i, j: (j, 0)),
            pl.BlockSpec((1, d), lambda i, j: (0, 0)),
        ],
        out_specs=pl.BlockSpec((tm, d), lambda i, j: (i, 0)),
        out_shape=jax.ShapeDtypeStruct((s, d), F32),
        scratch_shapes=[pltpu.VMEM((tm, d), BF16),
                        pltpu.VMEM((halo, d), BF16),
                        pltpu.VMEM((halo + tm, tf), F32)],
        compiler_params=_params("parallel", "arbitrary"),
        name="conv_ffn",
    )(x, x, g_in.reshape(1, d), w_up, w_up, w_conv, b_conv.reshape(1, d_ff),
      w_down, g_out.reshape(1, d))


def _sb_kernel(q_ref, k_ref, v_ref, o_ref, acc_ref, carry_ref, tri_ref, *, tq,
               heads):
    i = pl.program_id(1)
    scale = HEAD_DIM ** -0.5
    half = tq // 2
    hsl = [slice(h * HEAD_DIM, (h + 1) * HEAD_DIM) for h in range(heads)]

    @pl.when(i == 0)
    def _():
        rr = lax.broadcasted_iota(jnp.int32, (2 * half, 2 * half), 0)
        cc = lax.broadcasted_iota(jnp.int32, (2 * half, 2 * half), 1)
        tri_ref[...] = jnp.where((cc >= half) | ((rr & (half - 1)) > cc),
                                 1.0, 0.0).astype(BF16)

    tri = tri_ref[...]

    def tile(kstart, nk, causal):
        rows = pl.ds(kstart, nk)
        zs = [_dot_nt(q_ref[:, s], k_ref[rows, s]) * scale for s in hsl]
        softs = [jnp.log(1.0 + jnp.exp2(jnp.abs(z) * NEG_LOG2E)) for z in zs]
        log_sigs = [jnp.minimum(z, 0.0) - t for z, t in zip(zs, softs)]
        drops = [jnp.maximum(z, 0.0) + t for z, t in zip(zs, softs)]
        if causal is not None:
            drops = [jnp.where(causal, d, 0.0) for d in drops]
        his = [d.astype(BF16) for d in drops]
        los = [(d - hi.astype(F32)).astype(BF16) for d, hi in zip(drops, his)]
        carries = [carry_ref[:, s] for s in hsl]
        n_half = nk // half
        parts = [[None] * n_half for _ in hsl]
        for c in reversed(range(n_half)):
            sl = slice(c * half, (c + 1) * half)
            ress = [_dot(jnp.concatenate([hi[:, sl], lo[:, sl]], axis=1), tri)
                    for hi, lo in zip(his, los)]
            for h, res in enumerate(ress):
                parts[h][c] = jnp.exp(log_sigs[h][:, sl]
                                      - (carries[h] + res[:, :half]))
                carries[h] = carries[h] + res[:, half:]
        for h, s in enumerate(hsl):
            a = jnp.concatenate(parts[h], axis=1)
            if causal is not None:
                a = jnp.where(causal, a, 0.0)
            acc_ref[:, s] += _dot(a.astype(BF16), v_ref[rows, s])
            carry_ref[:, s] = carries[h]

    acc_ref[...] = jnp.zeros_like(acc_ref)
    carry_ref[...] = jnp.zeros_like(carry_ref)

    @pl.when(i == 0)
    def _():
        qi = lax.broadcasted_iota(jnp.int32, (tq, tq), 0)
        kj = lax.broadcasted_iota(jnp.int32, (tq, tq), 1)
        tile(0, tq, kj < qi)

    @pl.when(i > 0)
    def _():
        qi = lax.broadcasted_iota(jnp.int32, (tq, 2 * tq), 0)
        kj = lax.broadcasted_iota(jnp.int32, (tq, 2 * tq), 1)
        tile(pl.multiple_of((i - 1) * tq, tq), 2 * tq, kj < qi + tq)

    def live():
        return jnp.min(carry_ref[...]) < -EXP_UNDERFLOW_F32

    def cond(c):
        step, alive = c
        return (step < i) & alive

    def body(c):
        step, _ = c
        tile(pl.multiple_of((i - 1 - step) * tq, tq), tq, None)
        return step + 1, live()

    lax.while_loop(cond, body, (jnp.int32(1), live()))
    o_ref[...] = acc_ref[...].astype(o_ref.dtype)


def _stick_breaking(kvq, *, tq, heads):
    s = kvq.shape[0]
    assert tq == 2 * HEAD_DIM and s % tq == 0 and SB_HEADS % heads == 0
    groups = SB_HEADS // heads
    width = heads * HEAD_DIM
    whole = functools.partial(pl.BlockSpec, (s, width),
                              pipeline_mode=pl.Buffered(1))
    return pl.pallas_call(
        functools.partial(_sb_kernel, tq=tq, heads=heads),
        grid=(groups, s // tq),
        in_specs=[pl.BlockSpec((tq, width), lambda g, i: (i, 2 * groups + g)),
                  whole(lambda g, i: (0, g)),
                  whole(lambda g, i: (0, groups + g))],
        out_specs=pl.BlockSpec((tq, width), lambda g, i: (i, g)),
        out_shape=jax.ShapeDtypeStruct((s, SB_W), BF16),
        scratch_shapes=[pltpu.VMEM((tq, width), F32),
                        pltpu.VMEM((tq, width), F32),
                        pltpu.VMEM((tq, tq), BF16)],
        compiler_params=_params("arbitrary", "arbitrary"),
        name="stick_breaking",
    )(kvq, kvq, kvq)


def _rope_tables(s):
    pos = jnp.arange(s, dtype=F32)
    inv = ROPE_THETA ** (-jnp.arange(0, HEAD_DIM, 2, dtype=F32) / HEAD_DIM)
    ang = pos[:, None] * inv[None, :]
    cos, sin = jnp.cos(ang), jnp.sin(ang)
    return (jnp.concatenate([cos, cos], axis=-1),
            jnp.concatenate([-sin, sin], axis=-1))


def kernel(x, mem, norms, w_in_a, w_o_a, g_kv, w_kv, w_in_b, w_o_b, w_mem_kv,
           w_up, w_conv, b_conv, w_down):
    b, s, d = x.shape
    assert b == 1 and norms.shape[0] == 2
    xs = x.reshape(s, d)
    mems = mem.reshape(mem.shape[1], d)
    bf = lambda a: a.astype(BF16)
    tm = min(1024, s)

    nrm = norms[0]
    cos2, sin2 = _rope_tables(s)
    proj = _norm_matmul(xs, nrm[0:1], bf(w_in_a[0]), tm=tm, tn=GROUP_W,
                        out_dtype=F32, rope_cols=2 * DIL_W, cos2=cos2,
                        sin2=sin2)
    mkv = _norm_matmul(mems, nrm[4:5], bf(w_mem_kv[0]), tm=mems.shape[0],
                       tn=GROUP_W, out_dtype=BF16)
    outs, lses = [], []
    for gi, (_, dilation) in enumerate(DIL_GROUPS):
        o_g, l_g = _dilated_group(proj, gi, dilation)
        outs.append(o_g)
        lses.append(l_g)
    o_mem = _mem_attn(proj, 3 * DIL_W // MEM_W, mkv, tm=min(512, s))
    xs = _outproj_merge(outs, lses, o_mem, xs, bf(w_o_a[0]), nrm[1],
                        tm=min(512, s))
    xs = _ffn(xs, nrm[2], bf(w_up[0]), w_conv[0], b_conv[0], bf(w_down[0]),
              nrm[3], tm=min(1024, s), tf=512)

    nrm = norms[1]
    kvq = _norm_matmul(xs, jnp.stack([g_kv, nrm[0]]),
                       bf(jnp.concatenate([w_kv, w_in_b[0]], axis=1)),
                       tm=tm, tn=GROUP_W, out_dtype=BF16,
                       gain_split_col=2 * SB_W)
    mkv = _norm_matmul(mems, nrm[4:5], bf(w_mem_kv[1]), tm=mems.shape[0],
                       tn=GROUP_W, out_dtype=BF16)
    o_sb = _stick_breaking(kvq, tq=2 * HEAD_DIM, heads=4)
    o_mem = _mem_attn(kvq, 3 * SB_W // MEM_W, mkv, tm=min(512, s))
    xs = _outproj(o_sb, o_mem, xs, bf(w_o_b[0]), nrm[1], tm=min(512, s))
    xs = _ffn(xs, nrm[2], bf(w_up[1]), w_conv[1], b_conv[1], bf(w_down[1]),
              nrm[3], tm=min(1024, s), tf=512)
    return xs.reshape(b, s, d)
```

```python
import functools

import jax
import jax.numpy as jnp
from jax import lax
from jax.experimental import pallas as pl
from jax.experimental.pallas import tpu as pltpu

HEAD_DIM = 128
N_MEM = 256
MEM_HEADS = 4
DIL_GROUPS = ((128, 1), (512, 4), (2048, 16))
HEADS_PER_GROUP = 4
DIL_HEADS = len(DIL_GROUPS) * HEADS_PER_GROUP
SB_HEADS = 12
BLOCK = 128
CONV_WIDTH = 3
ROPE_THETA = 10000.0
EPS = 1e-6
NEG_INF = -1e30
EXP_UNDERFLOW_F32 = -105.0
NEG_LOG2E = -1.4426950408889634

DIL_W = DIL_HEADS * HEAD_DIM
MEM_W = MEM_HEADS * HEAD_DIM
SB_W = SB_HEADS * HEAD_DIM
GROUP_W = HEADS_PER_GROUP * HEAD_DIM
DIL_ROWS = BLOCK * max(d for _, d in DIL_GROUPS)
DIL_STAGE_CHAINS = {1: 16, 4: 16, 16: 8}

F32 = jnp.float32
BF16 = jnp.bfloat16

VMEM_LIMIT_BYTES = 56 * 1024 * 1024
SUBLANE_BF16 = 16


def _params(*sem):
    return pltpu.CompilerParams(dimension_semantics=sem,
                                vmem_limit_bytes=VMEM_LIMIT_BYTES)


def _rms(xf, g):
    y = xf * lax.rsqrt(jnp.mean(xf * xf, axis=-1, keepdims=True) + EPS)
    return y * g


def _dot(a, b):
    return jnp.dot(a, b, preferred_element_type=F32)


def _dot_nt(a, b):
    return lax.dot_general(a, b, (((1,), (1,)), ((), ())),
                           preferred_element_type=F32)


def _norm_rows(x_ref, g_ref, h_ref):
    xf = x_ref[...]
    y = xf * lax.rsqrt(jnp.mean(xf * xf, axis=-1, keepdims=True) + EPS)
    for gi in range(h_ref.shape[0]):
        h_ref[gi] = (y * g_ref[gi:gi + 1, :]).astype(BF16)


def _norm_matmul_kernel(*refs, n_tiles, rope_tiles, heads_per_tile, gain_split):
    if rope_tiles:
        x_ref, g_ref, w_ref, cos_ref, sin_ref, o_ref, h_ref, acc_ref = refs
    else:
        x_ref, g_ref, w_ref, o_ref, h_ref = refs
    j = pl.program_id(1)
    n_gains = h_ref.shape[0]

    def product():
        if n_gains == 1:
            lhs = h_ref[0]
        else:
            lhs = h_ref[jnp.where(j >= gain_split, 1, 0)]
        return _dot(lhs, w_ref[...])

    if not rope_tiles:
        @pl.when(j == 0)
        def _():
            _norm_rows(x_ref, g_ref, h_ref)

        o_ref[...] = product().astype(o_ref.dtype)
        return

    @pl.when(j == 0)
    def _():
        _norm_rows(x_ref, g_ref, h_ref)
        acc_ref[...] = jnp.zeros_like(acc_ref)

    def finish(tile):
        is_rope = tile < rope_tiles
        c = cos_ref[...]
        s = sin_ref[...]
        for hh in range(heads_per_tile):
            sl = slice(hh * HEAD_DIM, (hh + 1) * HEAD_DIM)
            seg = acc_ref[:, sl]
            rot = pltpu.roll(seg, HEAD_DIM // 2, 1)
            o_ref[:, sl] = jnp.where(is_rope, seg * c + rot * s,
                                     seg).astype(o_ref.dtype)

    @pl.when(j < n_tiles)
    def _():
        finish(j - 1)
        acc_ref[...] = product()

    @pl.when(j == n_tiles)
    def _():
        finish(n_tiles - 1)


def _norm_matmul(x, gains, w, *, tm, tn, out_dtype, gain_split_col=0,
                 rope_cols=0, cos2=None, sin2=None):
    m, k = x.shape
    n = w.shape[1]
    n_gains = gains.shape[0]
    assert m % tm == 0 and n % tn == 0
    assert rope_cols % tn == 0 and gain_split_col % tn == 0
    n_tiles = n // tn
    last = n_tiles - 1
    scratch = [pltpu.VMEM((n_gains, tm, k), BF16)]
    args = [x, gains, w]
    if rope_cols:
        steps = n_tiles + 1
        w_spec = pl.BlockSpec((k, tn), lambda i, j: (0, jnp.minimum(j, last)))
        out_spec = pl.BlockSpec((tm, tn),
                                lambda i, j: (i, jnp.maximum(j - 1, 0)))
        tables = [pl.BlockSpec((tm, HEAD_DIM), lambda i, j: (i, 0))] * 2
        args += [cos2, sin2]
        scratch.append(pltpu.VMEM((tm, tn), F32))
    else:
        steps = n_tiles
        w_spec = pl.BlockSpec((k, tn), lambda i, j: (0, j))
        out_spec = pl.BlockSpec((tm, tn), lambda i, j: (i, j))
        tables = []
    return pl.pallas_call(
        functools.partial(_norm_matmul_kernel, n_tiles=n_tiles,
                          rope_tiles=rope_cols // tn,
                          heads_per_tile=tn // HEAD_DIM,
                          gain_split=gain_split_col // tn),
        grid=(m // tm, steps),
        in_specs=[pl.BlockSpec((tm, k), lambda i, j: (i, 0)),
                  pl.BlockSpec((n_gains, k), lambda i, j: (0, 0)),
                  w_spec] + tables,
        out_specs=out_spec,
        out_shape=jax.ShapeDtypeStruct((m, n), out_dtype),
        scratch_shapes=scratch,
        compiler_params=_params("parallel", "arbitrary"),
        name="norm_matmul",
    )(*args)


def _dilated_kernel(q_ref, kc_ref, vc_ref, kp_ref, vp_ref, o_ref, l_ref, *,
                    dilation, chains_per_stage):
    n = pl.program_id(0)
    scale = HEAD_DIM ** -0.5
    nq = DIL_ROWS // (BLOCK * dilation)
    qi = lax.broadcasted_iota(jnp.int32, (BLOCK, BLOCK), 0)
    kj = lax.broadcasted_iota(jnp.int32, (BLOCK, BLOCK), 1)
    band = kj >= qi
    valid_cur = kj <= qi

    def rows(ref, b, r):
        start = b * BLOCK * dilation + r
        if dilation == 1:
            return pl.ds(start, BLOCK)
        return pl.ds(start, BLOCK, stride=dilation)

    loaded = {}

    def kv(b, r):
        if (b, r) not in loaded:
            if b < 0:
                idx = rows(kp_ref, 0, r)
                loaded[(b, r)] = (kp_ref[idx, :].astype(BF16),
                                  vp_ref[idx, :].astype(BF16))
            else:
                idx = rows(kc_ref, b, r)
                loaded[(b, r)] = (kc_ref[idx, :].astype(BF16),
                                  vc_ref[idx, :].astype(BF16))
        return loaded[(b, r)]

    def run_stage(chains):
        qs, kps, kcs, vps, vcs, masks, idxs = [], [], [], [], [], [], []
        for b, r in chains:
            idx = rows(q_ref, b, r)
            qs.append(q_ref[idx, :].astype(BF16))
            kp, vp = kv(b - 1, r)
            kc, vc = kv(b, r)
            kps.append(kp); vps.append(vp); kcs.append(kc); vcs.append(vc)
            masks.append(band if b > 0 else band & (n > 0))
            idxs.append(idx)
        sp = [jnp.where(m, _dot_nt(q, k) * scale, NEG_INF)
              for q, k, m in zip(qs, kps, masks)]
        sc = [jnp.where(valid_cur, _dot_nt(q, k) * scale, NEG_INF)
              for q, k in zip(qs, kcs)]
        mx = [jnp.maximum(jnp.max(a, axis=-1, keepdims=True),
                          jnp.max(c, axis=-1, keepdims=True))
              for a, c in zip(sp, sc)]
        tot = [jnp.sum(jnp.exp(a - m), axis=-1, keepdims=True)
               + jnp.sum(jnp.exp(c - m), axis=-1, keepdims=True)
               for a, c, m in zip(sp, sc, mx)]
        lse = [m + jnp.log(t) for m, t in zip(mx, tot)]
        pp = [jnp.exp(a - l).astype(BF16) for a, l in zip(sp, lse)]
        pc = [jnp.exp(c - l).astype(BF16) for c, l in zip(sc, lse)]
        for idx, a, c, vp, vc, l in zip(idxs, pp, pc, vps, vcs, lse):
            o_ref[idx, :] = _dot(a, vp) + _dot(c, vc)
            l_ref[idx, :] = jnp.broadcast_to(l, (BLOCK, HEAD_DIM))

    chains = [(b, r) for b in range(nq) for r in range(dilation)]
    for c0 in range(0, len(chains), chains_per_stage):
        run_stage(chains[c0:c0 + chains_per_stage])


def _dilated_group(proj, gi, dilation):
    s = proj.shape[0]
    assert s % DIL_ROWS == 0 and DIL_ROWS % (BLOCK * dilation) == 0
    prev_rows = BLOCK * dilation
    ratio = DIL_ROWS // prev_rows
    q_col = gi * HEADS_PER_GROUP
    k_col = DIL_HEADS + q_col
    v_col = 2 * DIL_HEADS + q_col

    def cur(col):
        return pl.BlockSpec((DIL_ROWS, HEAD_DIM), lambda n, h: (n, col + h))

    def prev(col):
        return pl.BlockSpec(
            (prev_rows, HEAD_DIM),
            lambda n, h: (jnp.maximum(n * ratio - 1, 0), col + h))

    out_spec = pl.BlockSpec((DIL_ROWS, HEAD_DIM), lambda n, h: (n, h))
    return pl.pallas_call(
        functools.partial(_dilated_kernel, dilation=dilation,
                          chains_per_stage=DIL_STAGE_CHAINS[dilation]),
        grid=(s // DIL_ROWS, HEADS_PER_GROUP),
        in_specs=[cur(q_col), cur(k_col), cur(v_col), prev(k_col),
                  prev(v_col)],
        out_specs=[out_spec, out_spec],
        out_shape=[jax.ShapeDtypeStruct((s, GROUP_W), F32)] * 2,
        compiler_params=_params("parallel", "arbitrary"),
        name=f"dilated_d{dilation}",
    )(proj, proj, proj, proj, proj)


def _mem_attn_kernel(q_ref, mkv_ref, o_ref):
    scale = HEAD_DIM ** -0.5
    for hh in range(MEM_HEADS):
        sl = slice(hh * HEAD_DIM, (hh + 1) * HEAD_DIM)
        vl = slice(MEM_W + hh * HEAD_DIM, MEM_W + (hh + 1) * HEAD_DIM)
        s = _dot_nt(q_ref[:, sl].astype(BF16), mkv_ref[:, sl]) * scale
        e = jnp.exp(s - jnp.max(s, axis=-1, keepdims=True))
        p = e / jnp.sum(e, axis=-1, keepdims=True)
        o_ref[:, sl] = _dot(p.astype(BF16), mkv_ref[:, vl]).astype(o_ref.dtype)


def _mem_attn(proj, col_tile, mkv, *, tm):
    s = proj.shape[0]
    n_mem = mkv.shape[0]
    return pl.pallas_call(
        _mem_attn_kernel,
        grid=(s // tm,),
        in_specs=[pl.BlockSpec((tm, MEM_W), lambda i: (i, col_tile)),
                  pl.BlockSpec((n_mem, 2 * MEM_W), lambda i: (0, 0))],
        out_specs=pl.BlockSpec((tm, MEM_W), lambda i: (i, 0)),
        out_shape=jax.ShapeDtypeStruct((s, MEM_W), BF16),
        compiler_params=_params("parallel"),
        name="mem_attn",
    )(proj, mkv)


def _outproj_merge_kernel(o1, o2, o3, l1, l2, l3, om_ref, x_ref, w_ref, g_ref,
                          out_ref):
    a1, a2, a3 = l1[...], l2[...], l3[...]
    mx = jnp.maximum(jnp.maximum(a1, a2), a3)
    e1, e2, e3 = jnp.exp(a1 - mx), jnp.exp(a2 - mx), jnp.exp(a3 - mx)
    den = e1 + e2 + e3
    o_dil = (e1 / den) * o1[...] + (e2 / den) * o2[...] + (e3 / den) * o3[...]
    heads = jnp.concatenate([o_dil.astype(BF16), om_ref[...]], axis=1)
    y = _dot(heads, w_ref[...])
    out_ref[...] = x_ref[...] + _rms(y, g_ref[...])


def _outproj_merge(outs, lses, o_mem, x, w_o, g, *, tm):
    s, d = x.shape
    row = lambda w: pl.BlockSpec((tm, w), lambda i: (i, 0))
    full = lambda a: pl.BlockSpec(a.shape, lambda i: (0, 0))
    g2 = g.reshape(1, d)
    return pl.pallas_call(
        _outproj_merge_kernel,
        grid=(s // tm,),
        in_specs=[row(GROUP_W)] * 6 + [row(MEM_W), row(d), full(w_o), full(g2)],
        out_specs=row(d),
        out_shape=jax.ShapeDtypeStruct((s, d), F32),
        compiler_params=_params("parallel"),
        name="outproj_merge",
    )(*outs, *lses, o_mem, x, w_o, g2)


def _outproj_kernel(oa_ref, om_ref, x_ref, w_ref, g_ref, out_ref):
    heads = jnp.concatenate([oa_ref[...], om_ref[...]], axis=1)
    y = _dot(heads, w_ref[...])
    out_ref[...] = x_ref[...] + _rms(y, g_ref[...])


def _outproj(o_attn, o_mem, x, w_o, g, *, tm):
    s, d = x.shape
    row = lambda w: pl.BlockSpec((tm, w), lambda i: (i, 0))
    full = lambda a: pl.BlockSpec(a.shape, lambda i: (0, 0))
    g2 = g.reshape(1, d)
    return pl.pallas_call(
        _outproj_kernel,
        grid=(s // tm,),
        in_specs=[row(o_attn.shape[1]), row(MEM_W), row(d), full(w_o),
                  full(g2)],
        out_specs=row(d),
        out_shape=jax.ShapeDtypeStruct((s, d), F32),
        compiler_params=_params("parallel"),
        name="outproj",
    )(o_attn, o_mem, x, w_o, g2)


def _ffn_kernel(x_ref, xh_ref, gin_ref, wg_ref, wv_ref, wc_ref, bc_ref, wd_ref,
                gout_ref, o_ref, hn_ref, hh_ref, gbuf_ref, *, tm):
    i = pl.program_id(0)
    j = pl.program_id(1)
    halo = SUBLANE_BF16

    @pl.when(j == 0)
    def _():
        _norm_rows(x_ref, gin_ref, hn_ref)
        hh_ref[...] = _rms(xh_ref[...], gin_ref[...]).astype(BF16)
        o_ref[...] = jnp.zeros_like(o_ref)

    wg = wg_ref[...]
    gate = _dot(hn_ref[0], wg)
    val = _dot(hn_ref[0], wv_ref[...])
    gate_halo = jnp.where(i > 0, _dot(hh_ref[...], wg), 0.0)
    gbuf_ref[0:halo, :] = gate_halo
    gbuf_ref[halo:halo + tm, :] = gate
    g_m1 = gbuf_ref[pl.ds(halo - 1, tm), :]
    g_m2 = gbuf_ref[pl.ds(halo - 2, tm), :]
    acc = bc_ref[...] + gate * wc_ref[2:3, :]
    acc = acc + g_m2 * wc_ref[0:1, :]
    acc = acc + g_m1 * wc_ref[1:2, :]
    act = (acc * (1.0 / (1.0 + jnp.exp(-acc)))) * val
    o_ref[...] += _dot(act.astype(BF16), wd_ref[...])

    @pl.when(j == pl.num_programs(1) - 1)
    def _():
        o_ref[...] = x_ref[...] + _rms(o_ref[...], gout_ref[...])


def _ffn(x, g_in, w_up, w_conv, b_conv, w_down, g_out, layer, *, tm, tf):
    s, d = x.shape
    depth, d_ff, _ = w_down.shape
    assert s % tm == 0 and d_ff % tf == 0 and tm % SUBLANE_BF16 == 0
    nf = d_ff // tf
    halo = SUBLANE_BF16
    return pl.pallas_call(
        functools.partial(_ffn_kernel, tm=tm),
        grid=(s // tm, nf),
        in_specs=[
            pl.BlockSpec((tm, d), lambda i, j: (i, 0),
                         pipeline_mode=pl.Buffered(1)),
            pl.BlockSpec((halo, d),
                         lambda i, j: (jnp.maximum(i * (tm // halo) - 1, 0), 0)),
            pl.BlockSpec((1, d), lambda i, j: (0, 0)),
            pl.BlockSpec((None, d, tf), lambda i, j: (layer, 0, j)),
            pl.BlockSpec((None, d, tf), lambda i, j: (layer, 0, nf + j)),
            pl.BlockSpec((None, CONV_WIDTH, tf), lambda i, j: (layer, 0, j)),
            pl.BlockSpec((None, 1, tf), lambda i, j: (layer, 0, j)),
            pl.BlockSpec((None, tf, d), lambda i, j: (layer, j, 0)),
            pl.BlockSpec((1, d), lambda i, j: (0, 0)),
        ],
        out_specs=pl.BlockSpec((tm, d), lambda i, j: (i, 0)),
        out_shape=jax.ShapeDtypeStruct((s, d), F32),
        scratch_shapes=[pltpu.VMEM((1, tm, d), BF16),
                        pltpu.VMEM((halo, d), BF16),
                        pltpu.VMEM((halo + tm, tf), F32)],
        compiler_params=_params("parallel", "arbitrary"),
        name="conv_ffn",
    )(x, x, g_in.reshape(1, d), w_up, w_up, w_conv,
      b_conv.reshape(depth, 1, d_ff), w_down, g_out.reshape(1, d))


def _sb_kernel(q_ref, k_ref, v_ref, o_ref, acc_ref, carry_ref, tri_ref, *, tq,
               heads):
    i = pl.program_id(1)
    scale = HEAD_DIM ** -0.5
    half = tq // 2
    hsl = [slice(h * HEAD_DIM, (h + 1) * HEAD_DIM) for h in range(heads)]

    @pl.when(i == 0)
    def _():
        rr = lax.broadcasted_iota(jnp.int32, (2 * half, 2 * half), 0)
        cc = lax.broadcasted_iota(jnp.int32, (2 * half, 2 * half), 1)
        tri_ref[...] = jnp.where((cc >= half) | ((rr & (half - 1)) > cc),
                                 1.0, 0.0).astype(BF16)

    tri = tri_ref[...]

    def tile(kstart, nk, causal):
        rows = pl.ds(kstart, nk)
        zs = [_dot_nt(q_ref[:, s], k_ref[rows, s]) * scale for s in hsl]
        softs = [jnp.log(1.0 + jnp.exp2(jnp.abs(z) * NEG_LOG2E)) for z in zs]
        log_sigs = [jnp.minimum(z, 0.0) - t for z, t in zip(zs, softs)]
        drops = [jnp.maximum(z, 0.0) + t for z, t in zip(zs, softs)]
        if causal is not None:
            drops = [jnp.where(causal, d, 0.0) for d in drops]
        his = [d.astype(BF16) for d in drops]
        los = [(d - hi.astype(F32)).astype(BF16) for d, hi in zip(drops, his)]
        carries = [carry_ref[:, s] for s in hsl]
        n_half = nk // half
        parts = [[None] * n_half for _ in hsl]
        for c in reversed(range(n_half)):
            sl = slice(c * half, (c + 1) * half)
            ress = [_dot(jnp.concatenate([hi[:, sl], lo[:, sl]], axis=1), tri)
                    for hi, lo in zip(his, los)]
            for h, res in enumerate(ress):
                parts[h][c] = jnp.exp(log_sigs[h][:, sl]
                                      - (carries[h] + res[:, :half]))
                carries[h] = carries[h] + res[:, half:]
        for h, s in enumerate(hsl):
            a = jnp.concatenate(parts[h], axis=1)
            if causal is not None:
                a = jnp.where(causal, a, 0.0)
            acc_ref[:, s] += _dot(a.astype(BF16), v_ref[rows, s])
            carry_ref[:, s] = carries[h]

    acc_ref[...] = jnp.zeros_like(acc_ref)
    carry_ref[...] = jnp.zeros_like(carry_ref)

    @pl.when(i == 0)
    def _():
        qi = lax.broadcasted_iota(jnp.int32, (tq, tq), 0)
        kj = lax.broadcasted_iota(jnp.int32, (tq, tq), 1)
        tile(0, tq, kj < qi)

    @pl.when(i > 0)
    def _():
        qi = lax.broadcasted_iota(jnp.int32, (tq, 2 * tq), 0)
        kj = lax.broadcasted_iota(jnp.int32, (tq, 2 * tq), 1)
        tile(pl.multiple_of((i - 1) * tq, tq), 2 * tq, kj < qi + tq)

    def live():
        return jnp.min(carry_ref[...]) < -EXP_UNDERFLOW_F32

    def cond(c):
        step, alive = c
        return (step < i) & alive

    def body(c):
        step, _ = c
        tile(pl.multiple_of((i - 1 - step) * tq, tq), tq, None)
        return step + 1, live()

    lax.while_loop(cond, body, (jnp.int32(1), live()))
    o_ref[...] = acc_ref[...].astype(o_ref.dtype)


def _stick_breaking(kvq, *, tq, heads):
    s = kvq.shape[0]
    assert tq == 2 * HEAD_DIM and s % tq == 0 and SB_HEADS % heads == 0
    groups = SB_HEADS // heads
    width = heads * HEAD_DIM
    whole = functools.partial(pl.BlockSpec, (s, width),
                              pipeline_mode=pl.Buffered(1))
    return pl.pallas_call(
        functools.partial(_sb_kernel, tq=tq, heads=heads),
        grid=(groups, s // tq),
        in_specs=[pl.BlockSpec((tq, width), lambda g, i: (i, 2 * groups + g)),
                  whole(lambda g, i: (0, g)),
                  whole(lambda g, i: (0, groups + g))],
        out_specs=pl.BlockSpec((tq, width), lambda g, i: (i, g)),
        out_shape=jax.ShapeDtypeStruct((s, SB_W), BF16),
        scratch_shapes=[pltpu.VMEM((tq, width), F32),
                        pltpu.VMEM((tq, width), F32),
                        pltpu.VMEM((tq, tq), BF16)],
        compiler_params=_params("arbitrary", "arbitrary"),
        name="stick_breaking",
    )(kvq, kvq, kvq)


def _rope_tables(s):
    pos = jnp.arange(s, dtype=F32)
    inv = ROPE_THETA ** (-jnp.arange(0, HEAD_DIM, 2, dtype=F32) / HEAD_DIM)
    ang = pos[:, None] * inv[None, :]
    cos, sin = jnp.cos(ang), jnp.sin(ang)
    return (jnp.concatenate([cos, cos], axis=-1),
            jnp.concatenate([-sin, sin], axis=-1))


def kernel(x, mem, norms, w_in_a, w_o_a, g_kv, w_kv, w_in_b, w_o_b, w_mem_kv,
           w_up, w_conv, b_conv, w_down):
    b, s, d = x.shape
    assert b == 1 and norms.shape[0] == 2
    xs = x.reshape(s, d)
    mems = mem.reshape(mem.shape[1], d)
    bf = lambda a: a.astype(BF16)
    tm = min(1024, s)

    nrm = norms[0]
    cos2, sin2 = _rope_tables(s)
    proj = _norm_matmul(xs, nrm[0:1], bf(w_in_a[0]), tm=tm, tn=GROUP_W,
                        out_dtype=F32, rope_cols=2 * DIL_W, cos2=cos2,
                        sin2=sin2)
    mkv = _norm_matmul(mems, nrm[4:5], bf(w_mem_kv[0]), tm=mems.shape[0],
                       tn=GROUP_W, out_dtype=BF16)
    outs, lses = [], []
    for gi, (_, dilation) in enumerate(DIL_GROUPS):
        o_g, l_g = _dilated_group(proj, gi, dilation)
        outs.append(o_g)
        lses.append(l_g)
    o_mem = _mem_attn(proj, 3 * DIL_W // MEM_W, mkv, tm=min(512, s))
    xs = _outproj_merge(outs, lses, o_mem, xs, bf(w_o_a[0]), nrm[1],
                        tm=min(512, s))
    w_up_bf, w_down_bf = bf(w_up), bf(w_down)
    xs = _ffn(xs, nrm[2], w_up_bf, w_conv, b_conv, w_down_bf, nrm[3], 0,
              tm=min(1024, s), tf=512)

    nrm = norms[1]
    kvq = _norm_matmul(xs, jnp.stack([g_kv, nrm[0]]),
                       bf(jnp.concatenate([w_kv, w_in_b[0]], axis=1)),
                       tm=tm, tn=GROUP_W, out_dtype=BF16,
                       gain_split_col=2 * SB_W)
    mkv = _norm_matmul(mems, nrm[4:5], bf(w_mem_kv[1]), tm=mems.shape[0],
                       tn=GROUP_W, out_dtype=BF16)
    o_sb = _stick_breaking(kvq, tq=2 * HEAD_DIM, heads=4)
    o_mem = _mem_attn(kvq, 3 * SB_W // MEM_W, mkv, tm=min(512, s))
    xs = _outproj(o_sb, o_mem, xs, bf(w_o_b[0]), nrm[1], tm=min(512, s))
    xs = _ffn(xs, nrm[2], w_up_bf, w_conv, b_conv, w_down_bf, nrm[3], 1,
              tm=min(1024, s), tf=512)
    return xs.reshape(b, s, d)
```

```python
import functools

import jax
import jax.numpy as jnp
from jax import lax
from jax.experimental import pallas as pl
from jax.experimental.pallas import tpu as pltpu

HEAD_DIM = 128
N_MEM = 256
MEM_HEADS = 4
DIL_GROUPS = ((128, 1), (512, 4), (2048, 16))
HEADS_PER_GROUP = 4
DIL_HEADS = len(DIL_GROUPS) * HEADS_PER_GROUP
SB_HEADS = 12
BLOCK = 128
CONV_WIDTH = 3
ROPE_THETA = 10000.0
EPS = 1e-6
NEG_INF = -1e30
EXP_UNDERFLOW_F32 = -105.0
NEG_LOG2E = -1.4426950408889634

DIL_W = DIL_HEADS * HEAD_DIM
MEM_W = MEM_HEADS * HEAD_DIM
SB_W = SB_HEADS * HEAD_DIM
GROUP_W = HEADS_PER_GROUP * HEAD_DIM
DIL_ROWS = BLOCK * max(d for _, d in DIL_GROUPS)
DIL_STAGE_CHAINS = {1: 16, 4: 16, 16: 8}

F32 = jnp.float32
BF16 = jnp.bfloat16

VMEM_LIMIT_BYTES = 56 * 1024 * 1024
SUBLANE_BF16 = 16


def _params(*sem):
    return pltpu.CompilerParams(dimension_semantics=sem,
                                vmem_limit_bytes=VMEM_LIMIT_BYTES)


def _rms(xf, g):
    y = xf * lax.rsqrt(jnp.mean(xf * xf, axis=-1, keepdims=True) + EPS)
    return y * g


def _dot(a, b):
    return jnp.dot(a, b, preferred_element_type=F32)


def _dot_nt(a, b):
    return lax.dot_general(a, b, (((1,), (1,)), ((), ())),
                           preferred_element_type=F32)


def _norm_rows(x_ref, g_ref, h_ref):
    xf = x_ref[...]
    y = xf * lax.rsqrt(jnp.mean(xf * xf, axis=-1, keepdims=True) + EPS)
    for gi in range(h_ref.shape[0]):
        h_ref[gi] = (y * g_ref[gi:gi + 1, :]).astype(BF16)


def _norm_matmul_kernel(*refs, rope_tiles, heads_per_tile, gain_split):
    if rope_tiles:
        x_ref, g_ref, w_ref, cos_ref, sin_ref, o_ref, h_ref = refs
    else:
        x_ref, g_ref, w_ref, o_ref, h_ref = refs
    j = pl.program_id(1)
    n_gains = h_ref.shape[0]

    @pl.when(j == 0)
    def _():
        _norm_rows(x_ref, g_ref, h_ref)

    if n_gains == 1:
        lhs = h_ref[0]
    else:
        lhs = h_ref[jnp.where(j >= gain_split, 1, 0)]
    acc = _dot(lhs, w_ref[...])

    if rope_tiles:
        is_rope = j < rope_tiles
        c = cos_ref[...]
        s = sin_ref[...]
        for hh in range(heads_per_tile):
            sl = slice(hh * HEAD_DIM, (hh + 1) * HEAD_DIM)
            seg = acc[:, sl]
            rot = pltpu.roll(seg, HEAD_DIM // 2, 1)
            o_ref[:, sl] = jnp.where(is_rope, seg * c + rot * s,
                                     seg).astype(o_ref.dtype)
    else:
        o_ref[...] = acc.astype(o_ref.dtype)


def _norm_matmul(x, gains, w, *, tm, tn, out_dtype, gain_split_col=0,
                 rope_cols=0, cos2=None, sin2=None):
    m, k = x.shape
    n = w.shape[1]
    n_gains = gains.shape[0]
    assert m % tm == 0 and n % tn == 0
    assert rope_cols % tn == 0 and gain_split_col % tn == 0
    in_specs = [
        pl.BlockSpec((tm, k), lambda i, j: (i, 0)),
        pl.BlockSpec((n_gains, k), lambda i, j: (0, 0)),
        pl.BlockSpec((k, tn), lambda i, j: (0, j)),
    ]
    args = [x, gains, w]
    if rope_cols:
        in_specs += [pl.BlockSpec((tm, HEAD_DIM), lambda i, j: (i, 0))] * 2
        args += [cos2, sin2]
    return pl.pallas_call(
        functools.partial(_norm_matmul_kernel, rope_tiles=rope_cols // tn,
                          heads_per_tile=tn // HEAD_DIM,
                          gain_split=gain_split_col // tn),
        grid=(m // tm, n // tn),
        in_specs=in_specs,
        out_specs=pl.BlockSpec((tm, tn), lambda i, j: (i, j)),
        out_shape=jax.ShapeDtypeStruct((m, n), out_dtype),
        scratch_shapes=[pltpu.VMEM((n_gains, tm, k), BF16)],
        compiler_params=_params("parallel", "arbitrary"),
        name="norm_matmul",
    )(*args)


def _dilated_kernel(q_ref, kc_ref, vc_ref, kp_ref, vp_ref, o_ref, l_ref, *,
                    dilation, chains_per_stage):
    n = pl.program_id(0)
    scale = HEAD_DIM ** -0.5
    nq = DIL_ROWS // (BLOCK * dilation)
    qi = lax.broadcasted_iota(jnp.int32, (BLOCK, BLOCK), 0)
    kj = lax.broadcasted_iota(jnp.int32, (BLOCK, BLOCK), 1)
    band = kj >= qi
    valid_cur = kj <= qi

    def rows(ref, b, r):
        start = b * BLOCK * dilation + r
        if dilation == 1:
            return pl.ds(start, BLOCK)
        return pl.ds(start, BLOCK, stride=dilation)

    loaded = {}

    def kv(b, r):
        if (b, r) not in loaded:
            if b < 0:
                idx = rows(kp_ref, 0, r)
                loaded[(b, r)] = (kp_ref[idx, :].astype(BF16),
                                  vp_ref[idx, :].astype(BF16))
            else:
                idx = rows(kc_ref, b, r)
                loaded[(b, r)] = (kc_ref[idx, :].astype(BF16),
                                  vc_ref[idx, :].astype(BF16))
        return loaded[(b, r)]

    def run_stage(chains):
        qs, kps, kcs, vps, vcs, masks, idxs = [], [], [], [], [], [], []
        for b, r in chains:
            idx = rows(q_ref, b, r)
            qs.append(q_ref[idx, :].astype(BF16))
            kp, vp = kv(b - 1, r)
            kc, vc = kv(b, r)
            kps.append(kp); vps.append(vp); kcs.append(kc); vcs.append(vc)
            masks.append(band if b > 0 else band & (n > 0))
            idxs.append(idx)
        sp = [jnp.where(m, _dot_nt(q, k) * scale, NEG_INF)
              for q, k, m in zip(qs, kps, masks)]
        sc = [jnp.where(valid_cur, _dot_nt(q, k) * scale, NEG_INF)
              for q, k in zip(qs, kcs)]
        mx = [jnp.maximum(jnp.max(a, axis=-1, keepdims=True),
                          jnp.max(c, axis=-1, keepdims=True))
              for a, c in zip(sp, sc)]
        tot = [jnp.sum(jnp.exp(a - m), axis=-1, keepdims=True)
               + jnp.sum(jnp.exp(c - m), axis=-1, keepdims=True)
               for a, c, m in zip(sp, sc, mx)]
        lse = [m + jnp.log(t) for m, t in zip(mx, tot)]
        pp = [jnp.exp(a - l).astype(BF16) for a, l in zip(sp, lse)]
        pc = [jnp.exp(c - l).astype(BF16) for c, l in zip(sc, lse)]
        for idx, a, c, vp, vc, l in zip(idxs, pp, pc, vps, vcs, lse):
            o_ref[idx, :] = _dot(a, vp) + _dot(c, vc)
            l_ref[idx, :] = jnp.broadcast_to(l, (BLOCK, HEAD_DIM))

    chains = [(b, r) for b in range(nq) for r in range(dilation)]
    for c0 in range(0, len(chains), chains_per_stage):
        run_stage(chains[c0:c0 + chains_per_stage])


def _dilated_group(proj, gi, dilation):
    s = proj.shape[0]
    assert s % DIL_ROWS == 0 and DIL_ROWS % (BLOCK * dilation) == 0
    prev_rows = BLOCK * dilation
    ratio = DIL_ROWS // prev_rows
    q_col = gi * HEADS_PER_GROUP
    k_col = DIL_HEADS + q_col
    v_col = 2 * DIL_HEADS + q_col

    def cur(col):
        return pl.BlockSpec((DIL_ROWS, HEAD_DIM), lambda n, h: (n, col + h))

    def prev(col):
        return pl.BlockSpec(
            (prev_rows, HEAD_DIM),
            lambda n, h: (jnp.maximum(n * ratio - 1, 0), col + h))

    out_spec = pl.BlockSpec((DIL_ROWS, HEAD_DIM), lambda n, h: (n, h))
    return pl.pallas_call(
        functools.partial(_dilated_kernel, dilation=dilation,
                          chains_per_stage=DIL_STAGE_CHAINS[dilation]),
        grid=(s // DIL_ROWS, HEADS_PER_GROUP),
        in_specs=[cur(q_col), cur(k_col), cur(v_col), prev(k_col),
                  prev(v_col)],
        out_specs=[out_spec, out_spec],
        out_shape=[jax.ShapeDtypeStruct((s, GROUP_W), F32)] * 2,
        compiler_params=_params("parallel", "arbitrary"),
        name=f"dilated_d{dilation}",
    )(proj, proj, proj, proj, proj)


def _mem_attn_kernel(q_ref, mkv_ref, o_ref):
    scale = HEAD_DIM ** -0.5
    for hh in range(MEM_HEADS):
        sl = slice(hh * HEAD_DIM, (hh + 1) * HEAD_DIM)
        vl = slice(MEM_W + hh * HEAD_DIM, MEM_W + (hh + 1) * HEAD_DIM)
        s = _dot_nt(q_ref[:, sl].astype(BF16), mkv_ref[:, sl]) * scale
        e = jnp.exp(s - jnp.max(s, axis=-1, keepdims=True))
        p = e / jnp.sum(e, axis=-1, keepdims=True)
        o_ref[:, sl] = _dot(p.astype(BF16), mkv_ref[:, vl]).astype(o_ref.dtype)


def _mem_attn(proj, col_tile, mkv, *, tm):
    s = proj.shape[0]
    n_mem = mkv.shape[0]
    return pl.pallas_call(
        _mem_attn_kernel,
        grid=(s // tm,),
        in_specs=[pl.BlockSpec((tm, MEM_W), lambda i: (i, col_tile)),
                  pl.BlockSpec((n_mem, 2 * MEM_W), lambda i: (0, 0))],
        out_specs=pl.BlockSpec((tm, MEM_W), lambda i: (i, 0)),
        out_shape=jax.ShapeDtypeStruct((s, MEM_W), BF16),
        compiler_params=_params("parallel"),
        name="mem_attn",
    )(proj, mkv)


def _outproj_merge_kernel(o1, o2, o3, l1, l2, l3, om_ref, x_ref, w_ref, g_ref,
                          out_ref):
    a1, a2, a3 = l1[...], l2[...], l3[...]
    mx = jnp.maximum(jnp.maximum(a1, a2), a3)
    e1, e2, e3 = jnp.exp(a1 - mx), jnp.exp(a2 - mx), jnp.exp(a3 - mx)
    den = e1 + e2 + e3
    o_dil = (e1 / den) * o1[...] + (e2 / den) * o2[...] + (e3 / den) * o3[...]
    heads = jnp.concatenate([o_dil.astype(BF16), om_ref[...]], axis=1)
    y = _dot(heads, w_ref[...])
    out_ref[...] = x_ref[...] + _rms(y, g_ref[...])


def _outproj_merge(outs, lses, o_mem, x, w_o, g, *, tm):
    s, d = x.shape
    row = lambda w: pl.BlockSpec((tm, w), lambda i: (i, 0))
    full = lambda a: pl.BlockSpec(a.shape, lambda i: (0, 0))
    g2 = g.reshape(1, d)
    return pl.pallas_call(
        _outproj_merge_kernel,
        grid=(s // tm,),
        in_specs=[row(GROUP_W)] * 6 + [row(MEM_W), row(d), full(w_o), full(g2)],
        out_specs=row(d),
        out_shape=jax.ShapeDtypeStruct((s, d), F32),
        compiler_params=_params("parallel"),
        name="outproj_merge",
    )(*outs, *lses, o_mem, x, w_o, g2)


def _outproj_kernel(oa_ref, om_ref, x_ref, w_ref, g_ref, out_ref):
    heads = jnp.concatenate([oa_ref[...], om_ref[...]], axis=1)
    y = _dot(heads, w_ref[...])
    out_ref[...] = x_ref[...] + _rms(y, g_ref[...])


def _outproj(o_attn, o_mem, x, w_o, g, *, tm):
    s, d = x.shape
    row = lambda w: pl.BlockSpec((tm, w), lambda i: (i, 0))
    full = lambda a: pl.BlockSpec(a.shape, lambda i: (0, 0))
    g2 = g.reshape(1, d)
    return pl.pallas_call(
        _outproj_kernel,
        grid=(s // tm,),
        in_specs=[row(o_attn.shape[1]), row(MEM_W), row(d), full(w_o),
                  full(g2)],
        out_specs=row(d),
        out_shape=jax.ShapeDtypeStruct((s, d), F32),
        compiler_params=_params("parallel"),
        name="outproj",
    )(o_attn, o_mem, x, w_o, g2)


def _ffn_kernel(x_ref, xh_ref, gin_ref, wg_ref, wv_ref, wc_ref, bc_ref, wd_ref,
                gout_ref, o_ref, hn_ref, hh_ref, gbuf_ref, *, tm):
    i = pl.program_id(0)
    j = pl.program_id(1)
    halo = SUBLANE_BF16

    @pl.when(j == 0)
    def _():
        _norm_rows(x_ref, gin_ref, hn_ref)
        hh_ref[...] = _rms(xh_ref[...], gin_ref[...]).astype(BF16)
        o_ref[...] = jnp.zeros_like(o_ref)

    wg = wg_ref[...]
    gate = _dot(hn_ref[0], wg)
    val = _dot(hn_ref[0], wv_ref[...])
    gate_halo = jnp.where(i > 0, _dot(hh_ref[...], wg), 0.0)
    gbuf_ref[0:halo, :] = gate_halo
    gbuf_ref[halo:halo + tm, :] = gate
    g_m1 = gbuf_ref[pl.ds(halo - 1, tm), :]
    g_m2 = gbuf_ref[pl.ds(halo - 2, tm), :]
    acc = bc_ref[...] + gate * wc_ref[2:3, :]
    acc = acc + g_m2 * wc_ref[0:1, :]
    acc = acc + g_m1 * wc_ref[1:2, :]
    act = (acc * (1.0 / (1.0 + jnp.exp(-acc)))) * val
    o_ref[...] += _dot(act.astype(BF16), wd_ref[...])

    @pl.when(j == pl.num_programs(1) - 1)
    def _():
        o_ref[...] = x_ref[...] + _rms(o_ref[...], gout_ref[...])


def _ffn(x, g_in, w_up, w_conv, b_conv, w_down, g_out, layer, *, tm, tf):
    s, d = x.shape
    depth, d_ff, _ = w_down.shape
    assert s % tm == 0 and d_ff % tf == 0 and tm % SUBLANE_BF16 == 0
    nf = d_ff // tf
    halo = SUBLANE_BF16
    return pl.pallas_call(
        functools.partial(_ffn_kernel, tm=tm),
        grid=(s // tm, nf),
        in_specs=[
            pl.BlockSpec((tm, d), lambda i, j: (i, 0),
                         pipeline_mode=pl.Buffered(1)),
            pl.BlockSpec((halo, d),
                         lambda i, j: (jnp.maximum(i * (tm // halo) - 1, 0), 0)),
            pl.BlockSpec((1, d), lambda i, j: (0, 0)),
            pl.BlockSpec((None, d, tf), lambda i, j: (layer, 0, j)),
            pl.BlockSpec((None, d, tf), lambda i, j: (layer, 0, nf + j)),
            pl.BlockSpec((None, CONV_WIDTH, tf), lambda i, j: (layer, 0, j)),
            pl.BlockSpec((None, 1, tf), lambda i, j: (layer, 0, j)),
            pl.BlockSpec((None, tf, d), lambda i, j: (layer, j, 0)),
            pl.BlockSpec((1, d), lambda i, j: (0, 0)),
        ],
        out_specs=pl.BlockSpec((tm, d), lambda i, j: (i, 0)),
        out_shape=jax.ShapeDtypeStruct((s, d), F32),
        scratch_shapes=[pltpu.VMEM((1, tm, d), BF16),
                        pltpu.VMEM((halo, d), BF16),
                        pltpu.VMEM((halo + tm, tf), F32)],
        compiler_params=_params("parallel", "arbitrary"),
        name="conv_ffn",
    )(x, x, g_in.reshape(1, d), w_up, w_up, w_conv,
      b_conv.reshape(depth, 1, d_ff), w_down, g_out.reshape(1, d))


def _sb_kernel(q_ref, k_ref, v_ref, o_ref, acc_ref, carry_ref, tri_ref, *, tq,
               heads):
    i = pl.program_id(1)
    scale = HEAD_DIM ** -0.5
    half = tq // 2
    hsl = [slice(h * HEAD_DIM, (h + 1) * HEAD_DIM) for h in range(heads)]

    @pl.when(i == 0)
    def _():
        rr = lax.broadcasted_iota(jnp.int32, (2 * half, 2 * half), 0)
        cc = lax.broadcasted_iota(jnp.int32, (2 * half, 2 * half), 1)
        tri_ref[...] = jnp.where((cc >= half) | ((rr & (half - 1)) > cc),
                                 1.0, 0.0).astype(BF16)

    tri = tri_ref[...]
    qi = lax.broadcasted_iota(jnp.int32, (half, half), 0)
    kj = lax.broadcasted_iota(jnp.int32, (half, half), 1)
    strictly_earlier = kj < qi

    def run(segments, nrows, nk, diagonal_last):
        chains = [(slice(r0, r0 + nrows), pl.ds(ks, nk), s)
                  for r0, ks in segments for s in hsl]
        n_half = nk // half
        last = (n_half - 1) * half

        def mask_last(t):
            if not diagonal_last:
                return t
            tail = jnp.where(strictly_earlier, t[:, last:], 0.0)
            return tail if n_half == 1 else jnp.concatenate(
                [t[:, :last], tail], axis=1)

        zs = [_dot_nt(q_ref[r, s], k_ref[k, s]) * scale for r, k, s in chains]
        softs = [jnp.log(1.0 + jnp.exp2(jnp.abs(z) * NEG_LOG2E)) for z in zs]
        log_sigs = [jnp.minimum(z, 0.0) - t for z, t in zip(zs, softs)]
        drops = [mask_last(jnp.maximum(z, 0.0) + t) for z, t in zip(zs, softs)]
        his = [d.astype(BF16) for d in drops]
        los = [(d - hi.astype(F32)).astype(BF16) for d, hi in zip(drops, his)]
        carries = [carry_ref[r, s] for r, _, s in chains]
        parts = [[None] * n_half for _ in chains]
        for c in reversed(range(n_half)):
            sl = slice(c * half, (c + 1) * half)
            ress = [_dot(jnp.concatenate([hi[:, sl], lo[:, sl]], axis=1), tri)
                    for hi, lo in zip(his, los)]
            for n, res in enumerate(ress):
                parts[n][c] = jnp.exp(log_sigs[n][:, sl]
                                      - (carries[n] + res[:, :half]))
                carries[n] = carries[n] + res[:, half:]
        for n, (r, k, s) in enumerate(chains):
            a = mask_last(jnp.concatenate(parts[n], axis=1))
            acc_ref[r, s] += _dot(a.astype(BF16), v_ref[k, s])
            carry_ref[r, s] = carries[n]

    acc_ref[...] = jnp.zeros_like(acc_ref)
    carry_ref[...] = jnp.zeros_like(carry_ref)

    @pl.when(i == 0)
    def _():
        run([(0, 0)], half, half, True)
        run([(half, 0)], half, tq, True)

    @pl.when(i > 0)
    def _():
        k0 = pl.multiple_of((i - 1) * tq, tq)
        run([(0, k0), (half, k0 + half)], half, tq + half, True)

    def live():
        return jnp.min(carry_ref[...]) < -EXP_UNDERFLOW_F32

    alive = live()

    @pl.when((i > 0) & alive)
    def _():
        run([(half, pl.multiple_of((i - 1) * tq, tq))], half, half, False)

    def cond(c):
        step, alive = c
        return (step < i) & alive

    def body(c):
        step, _ = c
        run([(0, pl.multiple_of((i - 1 - step) * tq, tq))], tq, tq, False)
        return step + 1, live()

    lax.while_loop(cond, body, (jnp.int32(1), alive))
    o_ref[...] = acc_ref[...].astype(o_ref.dtype)


def _stick_breaking(kvq, *, tq, heads):
    s = kvq.shape[0]
    assert tq == 2 * HEAD_DIM and s % tq == 0 and SB_HEADS % heads == 0
    groups = SB_HEADS // heads
    width = heads * HEAD_DIM
    whole = functools.partial(pl.BlockSpec, (s, width),
                              pipeline_mode=pl.Buffered(1))
    return pl.pallas_call(
        functools.partial(_sb_kernel, tq=tq, heads=heads),
        grid=(groups, s // tq),
        in_specs=[pl.BlockSpec((tq, width), lambda g, i: (i, 2 * groups + g)),
                  whole(lambda g, i: (0, g)),
                  whole(lambda g, i: (0, groups + g))],
        out_specs=pl.BlockSpec((tq, width), lambda g, i: (i, g)),
        out_shape=jax.ShapeDtypeStruct((s, SB_W), BF16),
        scratch_shapes=[pltpu.VMEM((tq, width), F32),
                        pltpu.VMEM((tq, width), F32),
                        pltpu.VMEM((tq, tq), BF16)],
        compiler_params=_params("arbitrary", "arbitrary"),
        name="stick_breaking",
    )(kvq, kvq, kvq)


def _rope_tables(s):
    pos = jnp.arange(s, dtype=F32)
    inv = ROPE_THETA ** (-jnp.arange(0, HEAD_DIM, 2, dtype=F32) / HEAD_DIM)
    ang = pos[:, None] * inv[None, :]
    cos, sin = jnp.cos(ang), jnp.sin(ang)
    return (jnp.concatenate([cos, cos], axis=-1),
            jnp.concatenate([-sin, sin], axis=-1))


def kernel(x, mem, norms, w_in_a, w_o_a, g_kv, w_kv, w_in_b, w_o_b, w_mem_kv,
           w_up, w_conv, b_conv, w_down):
    b, s, d = x.shape
    assert b == 1 and norms.shape[0] == 2
    xs = x.reshape(s, d)
    mems = mem.reshape(mem.shape[1], d)
    bf = lambda a: a.astype(BF16)
    tm = min(1024, s)

    nrm = norms[0]
    cos2, sin2 = _rope_tables(s)
    proj = _norm_matmul(xs, nrm[0:1], bf(w_in_a[0]), tm=tm, tn=2 * GROUP_W,
                        out_dtype=F32, rope_cols=2 * DIL_W, cos2=cos2,
                        sin2=sin2)
    mkv = _norm_matmul(mems, nrm[4:5], bf(w_mem_kv[0]), tm=mems.shape[0],
                       tn=GROUP_W, out_dtype=BF16)
    outs, lses = [], []
    for gi, (_, dilation) in enumerate(DIL_GROUPS):
        o_g, l_g = _dilated_group(proj, gi, dilation)
        outs.append(o_g)
        lses.append(l_g)
    o_mem = _mem_attn(proj, 3 * DIL_W // MEM_W, mkv, tm=min(512, s))
    xs = _outproj_merge(outs, lses, o_mem, xs, bf(w_o_a[0]), nrm[1],
                        tm=min(512, s))
    w_up_bf, w_down_bf = bf(w_up), bf(w_down)
    xs = _ffn(xs, nrm[2], w_up_bf, w_conv, b_conv, w_down_bf, nrm[3], 0,
              tm=min(1024, s), tf=512)

    nrm = norms[1]
    kvq = _norm_matmul(xs, jnp.stack([g_kv, nrm[0]]),
                       bf(jnp.concatenate([w_kv, w_in_b[0]], axis=1)),
                       tm=tm, tn=2 * GROUP_W, out_dtype=BF16,
                       gain_split_col=2 * SB_W)
    mkv = _norm_matmul(mems, nrm[4:5], bf(w_mem_kv[1]), tm=mems.shape[0],
                       tn=GROUP_W, out_dtype=BF16)
    o_sb = _stick_breaking(kvq, tq=2 * HEAD_DIM, heads=4)
    o_mem = _mem_attn(kvq, 3 * SB_W // MEM_W, mkv, tm=min(512, s))
    xs = _outproj(o_sb, o_mem, xs, bf(w_o_b[0]), nrm[1], tm=min(512, s))
    xs = _ffn(xs, nrm[2], w_up_bf, w_conv, b_conv, w_down_bf, nrm[3], 1,
              tm=min(1024, s), tf=512)
    return xs.reshape(b, s, d)
```

```python
import functools

import jax
import jax.numpy as jnp
from jax import lax
from jax.experimental import pallas as pl
from jax.experimental.pallas import tpu as pltpu

HEAD_DIM = 128
N_MEM = 256
MEM_HEADS = 4
DIL_GROUPS = ((128, 1), (512, 4), (2048, 16))
HEADS_PER_GROUP = 4
DIL_HEADS = len(DIL_GROUPS) * HEADS_PER_GROUP
SB_HEADS = 12
BLOCK = 128
CONV_WIDTH = 3
ROPE_THETA = 10000.0
EPS = 1e-6
NEG_INF = -1e30
EXP_UNDERFLOW_F32 = -105.0
NEG_LOG2E = -1.4426950408889634

DIL_W = DIL_HEADS * HEAD_DIM
MEM_W = MEM_HEADS * HEAD_DIM
SB_W = SB_HEADS * HEAD_DIM
GROUP_W = HEADS_PER_GROUP * HEAD_DIM
DIL_ROWS = BLOCK * max(d for _, d in DIL_GROUPS)
DIL_STAGE_CHAINS = {1: 16, 4: 16, 16: 8}

F32 = jnp.float32
BF16 = jnp.bfloat16

VMEM_LIMIT_BYTES = 59 * 1024 * 1024
SUBLANE_BF16 = 16


def _params(*sem):
    return pltpu.CompilerParams(dimension_semantics=sem,
                                vmem_limit_bytes=VMEM_LIMIT_BYTES)


def _rms(xf, g):
    y = xf * lax.rsqrt(jnp.mean(xf * xf, axis=-1, keepdims=True) + EPS)
    return y * g


def _dot(a, b):
    return jnp.dot(a, b, preferred_element_type=F32)


def _dot_nt(a, b):
    return lax.dot_general(a, b, (((1,), (1,)), ((), ())),
                           preferred_element_type=F32)


def _norm_rows(x_ref, g_ref, h_ref):
    xf = x_ref[...]
    y = xf * lax.rsqrt(jnp.mean(xf * xf, axis=-1, keepdims=True) + EPS)
    for gi in range(h_ref.shape[0]):
        h_ref[gi] = (y * g_ref[gi:gi + 1, :]).astype(BF16)


def _norm_matmul_kernel(*refs, rope_tiles, heads_per_tile, gain_split):
    if rope_tiles:
        x_ref, g_ref, w_ref, cos_ref, sin_ref, o_ref, h_ref = refs
    else:
        x_ref, g_ref, w_ref, o_ref, h_ref = refs
    j = pl.program_id(1)
    n_gains = h_ref.shape[0]

    @pl.when(j == 0)
    def _():
        _norm_rows(x_ref, g_ref, h_ref)

    if n_gains == 1:
        lhs = h_ref[0]
    else:
        lhs = h_ref[jnp.where(j >= gain_split, 1, 0)]
    acc = _dot(lhs, w_ref[...])

    if rope_tiles:
        is_rope = j < rope_tiles
        c = cos_ref[...]
        s = sin_ref[...]
        for hh in range(heads_per_tile):
            sl = slice(hh * HEAD_DIM, (hh + 1) * HEAD_DIM)
            seg = acc[:, sl]
            rot = pltpu.roll(seg, HEAD_DIM // 2, 1)
            o_ref[:, sl] = jnp.where(is_rope, seg * c + rot * s,
                                     seg).astype(o_ref.dtype)
    else:
        o_ref[...] = acc.astype(o_ref.dtype)


def _norm_matmul(x, gains, w, *, tm, tn, out_dtype, gain_split_col=0,
                 rope_cols=0, cos2=None, sin2=None):
    m, k = x.shape
    n = w.shape[1]
    n_gains = gains.shape[0]
    assert m % tm == 0 and n % tn == 0
    assert rope_cols % tn == 0 and gain_split_col % tn == 0
    in_specs = [
        pl.BlockSpec((tm, k), lambda i, j: (i, 0)),
        pl.BlockSpec((n_gains, k), lambda i, j: (0, 0)),
        pl.BlockSpec((k, tn), lambda i, j: (0, j)),
    ]
    args = [x, gains, w]
    if rope_cols:
        in_specs += [pl.BlockSpec((tm, HEAD_DIM), lambda i, j: (i, 0))] * 2
        args += [cos2, sin2]
    return pl.pallas_call(
        functools.partial(_norm_matmul_kernel, rope_tiles=rope_cols // tn,
                          heads_per_tile=tn // HEAD_DIM,
                          gain_split=gain_split_col // tn),
        grid=(m // tm, n // tn),
        in_specs=in_specs,
        out_specs=pl.BlockSpec((tm, tn), lambda i, j: (i, j)),
        out_shape=jax.ShapeDtypeStruct((m, n), out_dtype),
        scratch_shapes=[pltpu.VMEM((n_gains, tm, k), BF16)],
        compiler_params=_params("parallel", "arbitrary"),
        name="norm_matmul",
    )(*args)


def _dilated_kernel(q_ref, kc_ref, vc_ref, kp_ref, vp_ref, o_ref, l_ref, *,
                    dilation, chains_per_stage):
    n = pl.program_id(0)
    scale = HEAD_DIM ** -0.5
    nq = DIL_ROWS // (BLOCK * dilation)
    qi = lax.broadcasted_iota(jnp.int32, (BLOCK, BLOCK), 0)
    kj = lax.broadcasted_iota(jnp.int32, (BLOCK, BLOCK), 1)
    band = kj >= qi
    valid_cur = kj <= qi

    def rows(ref, b, r):
        start = b * BLOCK * dilation + r
        if dilation == 1:
            return pl.ds(start, BLOCK)
        return pl.ds(start, BLOCK, stride=dilation)

    loaded = {}

    def kv(b, r):
        if (b, r) not in loaded:
            if b < 0:
                idx = rows(kp_ref, 0, r)
                loaded[(b, r)] = (kp_ref[idx, :].astype(BF16),
                                  vp_ref[idx, :].astype(BF16))
            else:
                idx = rows(kc_ref, b, r)
                loaded[(b, r)] = (kc_ref[idx, :].astype(BF16),
                                  vc_ref[idx, :].astype(BF16))
        return loaded[(b, r)]

    def run_stage(chains):
        qs, kps, kcs, vps, vcs, masks, idxs = [], [], [], [], [], [], []
        for b, r in chains:
            idx = rows(q_ref, b, r)
            qs.append(q_ref[idx, :].astype(BF16))
            kp, vp = kv(b - 1, r)
            kc, vc = kv(b, r)
            kps.append(kp); vps.append(vp); kcs.append(kc); vcs.append(vc)
            masks.append(band if b > 0 else band & (n > 0))
            idxs.append(idx)
        sp = [jnp.where(m, _dot_nt(q, k) * scale, NEG_INF)
              for q, k, m in zip(qs, kps, masks)]
        sc = [jnp.where(valid_cur, _dot_nt(q, k) * scale, NEG_INF)
              for q, k in zip(qs, kcs)]
        mx = [jnp.maximum(jnp.max(a, axis=-1, keepdims=True),
                          jnp.max(c, axis=-1, keepdims=True))
              for a, c in zip(sp, sc)]
        tot = [jnp.sum(jnp.exp(a - m), axis=-1, keepdims=True)
               + jnp.sum(jnp.exp(c - m), axis=-1, keepdims=True)
               for a, c, m in zip(sp, sc, mx)]
        lse = [m + jnp.log(t) for m, t in zip(mx, tot)]
        pp = [jnp.exp(a - l).astype(BF16) for a, l in zip(sp, lse)]
        pc = [jnp.exp(c - l).astype(BF16) for c, l in zip(sc, lse)]
        for idx, a, c, vp, vc, l in zip(idxs, pp, pc, vps, vcs, lse):
            o_ref[idx, :] = _dot(a, vp) + _dot(c, vc)
            l_ref[idx, :] = jnp.broadcast_to(l, (BLOCK, HEAD_DIM))

    chains = [(b, r) for b in range(nq) for r in range(dilation)]
    for c0 in range(0, len(chains), chains_per_stage):
        run_stage(chains[c0:c0 + chains_per_stage])


def _dilated_group(proj, gi, dilation):
    s = proj.shape[0]
    assert s % DIL_ROWS == 0 and DIL_ROWS % (BLOCK * dilation) == 0
    prev_rows = BLOCK * dilation
    ratio = DIL_ROWS // prev_rows
    q_col = gi * HEADS_PER_GROUP
    k_col = DIL_HEADS + q_col
    v_col = 2 * DIL_HEADS + q_col

    def cur(col):
        return pl.BlockSpec((DIL_ROWS, HEAD_DIM), lambda n, h: (n, col + h))

    def prev(col):
        return pl.BlockSpec(
            (prev_rows, HEAD_DIM),
            lambda n, h: (jnp.maximum(n * ratio - 1, 0), col + h))

    out_spec = pl.BlockSpec((DIL_ROWS, HEAD_DIM), lambda n, h: (n, h))
    return pl.pallas_call(
        functools.partial(_dilated_kernel, dilation=dilation,
                          chains_per_stage=DIL_STAGE_CHAINS[dilation]),
        grid=(s // DIL_ROWS, HEADS_PER_GROUP),
        in_specs=[cur(q_col), cur(k_col), cur(v_col), prev(k_col),
                  prev(v_col)],
        out_specs=[out_spec, out_spec],
        out_shape=[jax.ShapeDtypeStruct((s, GROUP_W), F32)] * 2,
        compiler_params=_params("parallel", "arbitrary"),
        name=f"dilated_d{dilation}",
    )(proj, proj, proj, proj, proj)


def _mem_attn(qm_ref, mkv_ref):
    scale = HEAD_DIM ** -0.5
    ss = [_dot_nt(qm_ref[:, hh * HEAD_DIM:(hh + 1) * HEAD_DIM].astype(BF16),
                  mkv_ref[:, hh * HEAD_DIM:(hh + 1) * HEAD_DIM]) * scale
          for hh in range(MEM_HEADS)]
    es = [jnp.exp(s - jnp.max(s, axis=-1, keepdims=True)) for s in ss]
    ps = [(e / jnp.sum(e, axis=-1, keepdims=True)).astype(BF16) for e in es]
    return [_dot(p, mkv_ref[:, MEM_W + hh * HEAD_DIM:
                            MEM_W + (hh + 1) * HEAD_DIM]).astype(BF16)
            for hh, p in enumerate(ps)]


def _outproj_merge_kernel(o1, o2, o3, l1, l2, l3, qm_ref, mkv_ref, x_ref, w_ref,
                          g_ref, out_ref):
    o_mem = _mem_attn(qm_ref, mkv_ref)
    a1, a2, a3 = l1[...], l2[...], l3[...]
    mx = jnp.maximum(jnp.maximum(a1, a2), a3)
    e1, e2, e3 = jnp.exp(a1 - mx), jnp.exp(a2 - mx), jnp.exp(a3 - mx)
    den = e1 + e2 + e3
    o_dil = (e1 / den) * o1[...] + (e2 / den) * o2[...] + (e3 / den) * o3[...]
    heads = jnp.concatenate([o_dil.astype(BF16)] + o_mem, axis=1)
    y = _dot(heads, w_ref[...])
    out_ref[...] = x_ref[...] + _rms(y, g_ref[...])


def _outproj_merge(outs, lses, proj, qm_tile, mkv, x, w_o, g, *, tm):
    s, d = x.shape
    row = lambda w: pl.BlockSpec((tm, w), lambda i: (i, 0))
    full = lambda a: pl.BlockSpec(a.shape, lambda i: (0, 0))
    g2 = g.reshape(1, d)
    return pl.pallas_call(
        _outproj_merge_kernel,
        grid=(s // tm,),
        in_specs=[row(GROUP_W)] * 6
        + [pl.BlockSpec((tm, MEM_W), lambda i: (i, qm_tile)), full(mkv),
           row(d), full(w_o), full(g2)],
        out_specs=row(d),
        out_shape=jax.ShapeDtypeStruct((s, d), F32),
        compiler_params=_params("parallel"),
        name="outproj_merge",
    )(*outs, *lses, proj, mkv, x, w_o, g2)


def _outproj_kernel(oa_ref, qm_ref, mkv_ref, x_ref, w_ref, g_ref, out_ref):
    heads = jnp.concatenate([oa_ref[...]] + _mem_attn(qm_ref, mkv_ref), axis=1)
    y = _dot(heads, w_ref[...])
    out_ref[...] = x_ref[...] + _rms(y, g_ref[...])


def _outproj(o_attn, proj, qm_tile, mkv, x, w_o, g, *, tm):
    s, d = x.shape
    row = lambda w: pl.BlockSpec((tm, w), lambda i: (i, 0))
    full = lambda a: pl.BlockSpec(a.shape, lambda i: (0, 0))
    g2 = g.reshape(1, d)
    return pl.pallas_call(
        _outproj_kernel,
        grid=(s // tm,),
        in_specs=[row(o_attn.shape[1]),
                  pl.BlockSpec((tm, MEM_W), lambda i: (i, qm_tile)), full(mkv),
                  row(d), full(w_o), full(g2)],
        out_specs=row(d),
        out_shape=jax.ShapeDtypeStruct((s, d), F32),
        compiler_params=_params("parallel"),
        name="outproj",
    )(o_attn, proj, mkv, x, w_o, g2)


def _ffn_kernel(x_ref, xh_ref, gin_ref, wg_ref, wv_ref, wc_ref, bc_ref, wd_ref,
                gout_ref, o_ref, hn_ref, hh_ref, gbuf_ref, act_ref, *, tm, nf):
    i = pl.program_id(0)
    j = pl.program_id(1)
    halo = SUBLANE_BF16

    def up(slot):
        wg = wg_ref[...]
        gate = _dot(hn_ref[0], wg)
        val = _dot(hn_ref[0], wv_ref[...])
        gate_halo = jnp.where(i > 0, _dot(hh_ref[...], wg), 0.0)
        gbuf_ref[0:halo, :] = gate_halo
        gbuf_ref[halo:halo + tm, :] = gate
        g_m1 = gbuf_ref[pl.ds(halo - 1, tm), :]
        g_m2 = gbuf_ref[pl.ds(halo - 2, tm), :]
        acc = bc_ref[...] + gate * wc_ref[2:3, :]
        acc = acc + g_m2 * wc_ref[0:1, :]
        acc = acc + g_m1 * wc_ref[1:2, :]
        act = (acc * (1.0 / (1.0 + jnp.exp(-acc)))) * val
        act_ref[slot] = act.astype(BF16)

    def down(slot):
        o_ref[...] += _dot(act_ref[slot], wd_ref[...])

    @pl.when(j == 0)
    def _():
        _norm_rows(x_ref, gin_ref, hn_ref)
        hh_ref[...] = _rms(xh_ref[...], gin_ref[...]).astype(BF16)
        o_ref[...] = jnp.zeros_like(o_ref)
        up(0)

    @pl.when((j > 0) & (j < nf))
    def _():
        slot = lax.rem(j, 2)
        down(1 - slot)
        up(slot)

    @pl.when(j == nf)
    def _():
        down((nf - 1) % 2)
        o_ref[...] = x_ref[...] + _rms(o_ref[...], gout_ref[...])


def _ffn(x, g_in, w_up, w_conv, b_conv, w_down, g_out, layer, *, tm, tf):
    s, d = x.shape
    depth, d_ff, _ = w_down.shape
    assert s % tm == 0 and d_ff % tf == 0 and tm % SUBLANE_BF16 == 0
    nf = d_ff // tf
    halo = SUBLANE_BF16
    up_j = lambda j: jnp.minimum(j, nf - 1)
    down_j = lambda j: jnp.maximum(j - 1, 0)
    return pl.pallas_call(
        functools.partial(_ffn_kernel, tm=tm, nf=nf),
        grid=(s // tm, nf + 1),
        in_specs=[
            pl.BlockSpec((tm, d), lambda i, j: (i, 0)),
            pl.BlockSpec((halo, d),
                         lambda i, j: (jnp.maximum(i * (tm // halo) - 1, 0), 0)),
            pl.BlockSpec((1, d), lambda i, j: (0, 0)),
            pl.BlockSpec((None, d, tf), lambda i, j: (layer, 0, up_j(j))),
            pl.BlockSpec((None, d, tf), lambda i, j: (layer, 0, nf + up_j(j))),
            pl.BlockSpec((None, CONV_WIDTH, tf),
                         lambda i, j: (layer, 0, up_j(j))),
            pl.BlockSpec((None, 1, tf), lambda i, j: (layer, 0, up_j(j))),
            pl.BlockSpec((None, tf, d), lambda i, j: (layer, down_j(j), 0)),
            pl.BlockSpec((1, d), lambda i, j: (0, 0)),
        ],
        out_specs=pl.BlockSpec((tm, d), lambda i, j: (i, 0)),
        out_shape=jax.ShapeDtypeStruct((s, d), F32),
        scratch_shapes=[pltpu.VMEM((1, tm, d), BF16),
                        pltpu.VMEM((halo, d), BF16),
                        pltpu.VMEM((halo + tm, tf), F32),
                        pltpu.VMEM((2, tm, tf), BF16)],
        compiler_params=_params("parallel", "arbitrary"),
        name="conv_ffn",
    )(x, x, g_in.reshape(1, d), w_up, w_up, w_conv,
      b_conv.reshape(depth, 1, d_ff), w_down, g_out.reshape(1, d))


def _sb_kernel(q_ref, k_ref, v_ref, o_ref, acc_ref, carry_ref, tri_ref, *, tq,
               heads):
    i = pl.program_id(1)
    scale = HEAD_DIM ** -0.5
    half = tq // 2
    hsl = [slice(h * HEAD_DIM, (h + 1) * HEAD_DIM) for h in range(heads)]

    @pl.when(i == 0)
    def _():
        rr = lax.broadcasted_iota(jnp.int32, (2 * half, 2 * half), 0)
        cc = lax.broadcasted_iota(jnp.int32, (2 * half, 2 * half), 1)
        tri_ref[...] = jnp.where((cc >= half) | ((rr & (half - 1)) > cc),
                                 1.0, 0.0).astype(BF16)

    tri = tri_ref[...]
    qi = lax.broadcasted_iota(jnp.int32, (half, half), 0)
    kj = lax.broadcasted_iota(jnp.int32, (half, half), 1)
    strictly_earlier = kj < qi

    def run(segments, nrows, nk, diagonal_last):
        chains = [(slice(r0, r0 + nrows), pl.ds(ks, nk), s)
                  for r0, ks in segments for s in hsl]
        n_half = nk // half
        last = (n_half - 1) * half

        def mask_last(t):
            if not diagonal_last:
                return t
            tail = jnp.where(strictly_earlier, t[:, last:], 0.0)
            return tail if n_half == 1 else jnp.concatenate(
                [t[:, :last], tail], axis=1)

        zs = [_dot_nt(q_ref[r, s], k_ref[k, s]) * scale for r, k, s in chains]
        softs = [jnp.log(1.0 + jnp.exp2(jnp.abs(z) * NEG_LOG2E)) for z in zs]
        log_sigs = [jnp.minimum(z, 0.0) - t for z, t in zip(zs, softs)]
        drops = [mask_last(jnp.maximum(z, 0.0) + t) for z, t in zip(zs, softs)]
        his = [d.astype(BF16) for d in drops]
        los = [(d - hi.astype(F32)).astype(BF16) for d, hi in zip(drops, his)]
        first = diagonal_last
        carries = [None if first else carry_ref[r, s] for r, _, s in chains]
        parts = [[None] * n_half for _ in chains]
        for c in reversed(range(n_half)):
            sl = slice(c * half, (c + 1) * half)
            ress = [_dot(jnp.concatenate([hi[:, sl], lo[:, sl]], axis=1), tri)
                    for hi, lo in zip(his, los)]
            for n, res in enumerate(ress):
                if carries[n] is None:
                    parts[n][c] = jnp.exp(log_sigs[n][:, sl] - res[:, :half])
                    carries[n] = res[:, half:]
                else:
                    parts[n][c] = jnp.exp(log_sigs[n][:, sl]
                                          - (carries[n] + res[:, :half]))
                    carries[n] = carries[n] + res[:, half:]
        for n, (r, k, s) in enumerate(chains):
            a = mask_last(jnp.concatenate(parts[n], axis=1))
            pv = _dot(a.astype(BF16), v_ref[k, s])
            if first:
                acc_ref[r, s] = pv
            else:
                acc_ref[r, s] += pv
            carry_ref[r, s] = carries[n]

    @pl.when(i == 0)
    def _():
        run([(0, 0)], half, half, True)
        run([(half, 0)], half, tq, True)

    @pl.when(i > 0)
    def _():
        k0 = pl.multiple_of((i - 1) * tq, tq)
        run([(0, k0), (half, k0 + half)], half, tq + half, True)

    def live():
        return jnp.min(carry_ref[...]) < -EXP_UNDERFLOW_F32

    alive = live()

    @pl.when((i > 0) & alive)
    def _():
        run([(half, pl.multiple_of((i - 1) * tq, tq))], half, half, False)

    def cond(c):
        step, alive = c
        return (step < i) & alive

    def body(c):
        step, _ = c
        run([(0, pl.multiple_of((i - 1 - step) * tq, tq))], tq, tq, False)
        return step + 1, live()

    lax.while_loop(cond, body, (jnp.int32(1), alive))
    o_ref[...] = acc_ref[...].astype(o_ref.dtype)


def _stick_breaking(kvq, *, tq, heads):
    s = kvq.shape[0]
    assert tq == 2 * HEAD_DIM and s % tq == 0 and SB_HEADS % heads == 0
    groups = SB_HEADS // heads
    width = heads * HEAD_DIM
    whole = functools.partial(pl.BlockSpec, (s, width),
                              pipeline_mode=pl.Buffered(1))
    return pl.pallas_call(
        functools.partial(_sb_kernel, tq=tq, heads=heads),
        grid=(groups, s // tq),
        in_specs=[pl.BlockSpec((tq, width), lambda g, i: (i, 2 * groups + g)),
                  whole(lambda g, i: (0, g)),
                  whole(lambda g, i: (0, groups + g))],
        out_specs=pl.BlockSpec((tq, width), lambda g, i: (i, g)),
        out_shape=jax.ShapeDtypeStruct((s, SB_W), BF16),
        scratch_shapes=[pltpu.VMEM((tq, width), F32),
                        pltpu.VMEM((tq, width), F32),
                        pltpu.VMEM((tq, tq), BF16)],
        compiler_params=_params("arbitrary", "arbitrary"),
        name="stick_breaking",
    )(kvq, kvq, kvq)


def _rope_tables(s):
    pos = jnp.arange(s, dtype=F32)
    inv = ROPE_THETA ** (-jnp.arange(0, HEAD_DIM, 2, dtype=F32) / HEAD_DIM)
    ang = pos[:, None] * inv[None, :]
    cos, sin = jnp.cos(ang), jnp.sin(ang)
    return (jnp.concatenate([cos, cos], axis=-1),
            jnp.concatenate([-sin, sin], axis=-1))


def kernel(x, mem, norms, w_in_a, w_o_a, g_kv, w_kv, w_in_b, w_o_b, w_mem_kv,
           w_up, w_conv, b_conv, w_down):
    b, s, d = x.shape
    assert b == 1 and norms.shape[0] == 2
    xs = x.reshape(s, d)
    mems = mem.reshape(mem.shape[1], d)
    bf = lambda a: a.astype(BF16)
    tm = min(1024, s)

    nrm = norms[0]
    cos2, sin2 = _rope_tables(s)
    proj = _norm_matmul(xs, nrm[0:1], bf(w_in_a[0]), tm=tm, tn=2 * GROUP_W,
                        out_dtype=F32, rope_cols=2 * DIL_W, cos2=cos2,
                        sin2=sin2)
    mkv = _norm_matmul(mems, nrm[4:5], bf(w_mem_kv[0]), tm=mems.shape[0],
                       tn=GROUP_W, out_dtype=BF16)
    outs, lses = [], []
    for gi, (_, dilation) in enumerate(DIL_GROUPS):
        o_g, l_g = _dilated_group(proj, gi, dilation)
        outs.append(o_g)
        lses.append(l_g)
    xs = _outproj_merge(outs, lses, proj, 3 * DIL_W // MEM_W, mkv, xs,
                        bf(w_o_a[0]), nrm[1], tm=min(512, s))
    w_up_bf, w_down_bf = bf(w_up), bf(w_down)
    xs = _ffn(xs, nrm[2], w_up_bf, w_conv, b_conv, w_down_bf, nrm[3], 0,
              tm=min(1024, s), tf=512)

    nrm = norms[1]
    kvq = _norm_matmul(xs, jnp.stack([g_kv, nrm[0]]),
                       bf(jnp.concatenate([w_kv, w_in_b[0]], axis=1)),
                       tm=tm, tn=2 * GROUP_W, out_dtype=BF16,
                       gain_split_col=2 * SB_W)
    mkv = _norm_matmul(mems, nrm[4:5], bf(w_mem_kv[1]), tm=mems.shape[0],
                       tn=GROUP_W, out_dtype=BF16)
    o_sb = _stick_breaking(kvq, tq=2 * HEAD_DIM, heads=4)
    xs = _outproj(o_sb, kvq, 3 * SB_W // MEM_W, mkv, xs, bf(w_o_b[0]), nrm[1],
                  tm=min(512, s))
    xs = _ffn(xs, nrm[2], w_up_bf, w_conv, b_conv, w_down_bf, nrm[3], 1,
              tm=min(1024, s), tf=512)
    return xs.reshape(b, s, d)
```

```python
import functools

import jax
import jax.numpy as jnp
from jax import lax
from jax.experimental import pallas as pl
from jax.experimental.pallas import tpu as pltpu

HEAD_DIM = 128
N_MEM = 256
MEM_HEADS = 4
DIL_GROUPS = ((128, 1), (512, 4), (2048, 16))
HEADS_PER_GROUP = 4
DIL_HEADS = len(DIL_GROUPS) * HEADS_PER_GROUP
SB_HEADS = 12
BLOCK = 128
CONV_WIDTH = 3
ROPE_THETA = 10000.0
EPS = 1e-6
NEG_INF = -1e30
EXP_UNDERFLOW_F32 = -105.0
NEG_LOG2E = -1.4426950408889634

DIL_W = DIL_HEADS * HEAD_DIM
MEM_W = MEM_HEADS * HEAD_DIM
SB_W = SB_HEADS * HEAD_DIM
GROUP_W = HEADS_PER_GROUP * HEAD_DIM
DIL_ROWS = BLOCK * max(d for _, d in DIL_GROUPS)
DIL_STAGE_CHAINS = {1: 16, 4: 16, 16: 8}

F32 = jnp.float32
BF16 = jnp.bfloat16

VMEM_LIMIT_BYTES = 59 * 1024 * 1024
SUBLANE_BF16 = 16


def _params(*sem):
    return pltpu.CompilerParams(dimension_semantics=sem,
                                vmem_limit_bytes=VMEM_LIMIT_BYTES)


def _rms(xf, g):
    y = xf * lax.rsqrt(jnp.mean(xf * xf, axis=-1, keepdims=True) + EPS)
    return y * g


def _dot(a, b):
    return jnp.dot(a, b, preferred_element_type=F32)


def _dot_nt(a, b):
    return lax.dot_general(a, b, (((1,), (1,)), ((), ())),
                           preferred_element_type=F32)


def _norm_rows(x_ref, g_ref, h_ref):
    xf = x_ref[...]
    y = xf * lax.rsqrt(jnp.mean(xf * xf, axis=-1, keepdims=True) + EPS)
    for gi in range(h_ref.shape[0]):
        h_ref[gi] = (y * g_ref[gi:gi + 1, :]).astype(BF16)


def _norm_matmul_kernel(*refs, rope_tiles, heads_per_tile, gain_split):
    if rope_tiles:
        x_ref, g_ref, w_ref, cos_ref, sin_ref, o_ref, h_ref = refs
    else:
        x_ref, g_ref, w_ref, o_ref, h_ref = refs
    j = pl.program_id(1)
    n_gains = h_ref.shape[0]

    @pl.when(j == 0)
    def _():
        _norm_rows(x_ref, g_ref, h_ref)

    if n_gains == 1:
        lhs = h_ref[0]
    else:
        lhs = h_ref[jnp.where(j >= gain_split, 1, 0)]
    acc = _dot(lhs, w_ref[...])

    if rope_tiles:
        is_rope = j < rope_tiles
        c = cos_ref[...]
        s = sin_ref[...]
        for hh in range(heads_per_tile):
            seg = acc[:, hh * HEAD_DIM:(hh + 1) * HEAD_DIM]
            rot = pltpu.roll(seg, HEAD_DIM // 2, 1)
            o_ref[hh] = jnp.where(is_rope, seg * c + rot * s,
                                  seg).astype(o_ref.dtype)
    else:
        o_ref[...] = acc.astype(o_ref.dtype)


def _norm_matmul(x, gains, w, *, tm, tn, out_dtype, gain_split_col=0,
                 rope_cols=0, cos2=None, sin2=None):
    m, k = x.shape
    n = w.shape[1]
    n_gains = gains.shape[0]
    assert m % tm == 0 and n % tn == 0
    assert rope_cols % tn == 0 and gain_split_col % tn == 0
    in_specs = [
        pl.BlockSpec((tm, k), lambda i, j: (i, 0)),
        pl.BlockSpec((n_gains, k), lambda i, j: (0, 0)),
        pl.BlockSpec((k, tn), lambda i, j: (0, j)),
    ]
    args = [x, gains, w]
    if rope_cols:
        in_specs += [pl.BlockSpec((tm, HEAD_DIM), lambda i, j: (i, 0))] * 2
        args += [cos2, sin2]
        out_spec = pl.BlockSpec((tn // HEAD_DIM, tm, HEAD_DIM),
                                lambda i, j: (j, i, 0))
        out_shape = jax.ShapeDtypeStruct((n // HEAD_DIM, m, HEAD_DIM), out_dtype)
    else:
        out_spec = pl.BlockSpec((tm, tn), lambda i, j: (i, j))
        out_shape = jax.ShapeDtypeStruct((m, n), out_dtype)
    return pl.pallas_call(
        functools.partial(_norm_matmul_kernel, rope_tiles=rope_cols // tn,
                          heads_per_tile=tn // HEAD_DIM,
                          gain_split=gain_split_col // tn),
        grid=(m // tm, n // tn),
        in_specs=in_specs,
        out_specs=out_spec,
        out_shape=out_shape,
        scratch_shapes=[pltpu.VMEM((n_gains, tm, k), BF16)],
        compiler_params=_params("parallel", "arbitrary"),
        name="norm_matmul",
    )(*args)


def _dilated_kernel(q_ref, kc_ref, vc_ref, kp_ref, vp_ref, o_ref, l_ref, *,
                    dilation, chains_per_stage):
    n = pl.program_id(0)
    scale = HEAD_DIM ** -0.5
    nq = DIL_ROWS // (BLOCK * dilation)
    qi = lax.broadcasted_iota(jnp.int32, (BLOCK, BLOCK), 0)
    kj = lax.broadcasted_iota(jnp.int32, (BLOCK, BLOCK), 1)
    band = kj >= qi
    valid_cur = kj <= qi

    def rows(ref, b, r):
        start = b * BLOCK * dilation + r
        if dilation == 1:
            return pl.ds(start, BLOCK)
        return pl.ds(start, BLOCK, stride=dilation)

    loaded = {}

    def kv(b, r):
        if (b, r) not in loaded:
            if b < 0:
                idx = rows(kp_ref, 0, r)
                loaded[(b, r)] = (kp_ref[idx, :].astype(BF16),
                                  vp_ref[idx, :].astype(BF16))
            else:
                idx = rows(kc_ref, b, r)
                loaded[(b, r)] = (kc_ref[idx, :].astype(BF16),
                                  vc_ref[idx, :].astype(BF16))
        return loaded[(b, r)]

    def run_stage(chains):
        qs, kps, kcs, vps, vcs, masks, idxs = [], [], [], [], [], [], []
        for b, r in chains:
            idx = rows(q_ref, b, r)
            qs.append(q_ref[idx, :].astype(BF16))
            kp, vp = kv(b - 1, r)
            kc, vc = kv(b, r)
            kps.append(kp); vps.append(vp); kcs.append(kc); vcs.append(vc)
            masks.append(band if b > 0 else band & (n > 0))
            idxs.append(idx)
        sp = [jnp.where(m, _dot_nt(q, k) * scale, NEG_INF)
              for q, k, m in zip(qs, kps, masks)]
        sc = [jnp.where(valid_cur, _dot_nt(q, k) * scale, NEG_INF)
              for q, k in zip(qs, kcs)]
        mx = [jnp.maximum(jnp.max(a, axis=-1, keepdims=True),
                          jnp.max(c, axis=-1, keepdims=True))
              for a, c in zip(sp, sc)]
        tot = [jnp.sum(jnp.exp(a - m), axis=-1, keepdims=True)
               + jnp.sum(jnp.exp(c - m), axis=-1, keepdims=True)
               for a, c, m in zip(sp, sc, mx)]
        lse = [m + jnp.log(t) for m, t in zip(mx, tot)]
        pp = [jnp.exp(a - l).astype(BF16) for a, l in zip(sp, lse)]
        pc = [jnp.exp(c - l).astype(BF16) for c, l in zip(sc, lse)]
        for idx, a, c, vp, vc, l in zip(idxs, pp, pc, vps, vcs, lse):
            o_ref[idx, :] = _dot(a, vp) + _dot(c, vc)
            l_ref[idx, :] = jnp.broadcast_to(l, (BLOCK, HEAD_DIM))

    chains = [(b, r) for b in range(nq) for r in range(dilation)]
    for c0 in range(0, len(chains), chains_per_stage):
        run_stage(chains[c0:c0 + chains_per_stage])


def _dilated_group(proj, gi, dilation):
    s = proj.shape[1]
    assert s % DIL_ROWS == 0 and DIL_ROWS % (BLOCK * dilation) == 0
    prev_rows = BLOCK * dilation
    ratio = DIL_ROWS // prev_rows
    q_col = gi * HEADS_PER_GROUP
    k_col = DIL_HEADS + q_col
    v_col = 2 * DIL_HEADS + q_col

    def cur(col):
        return pl.BlockSpec((None, DIL_ROWS, HEAD_DIM),
                            lambda n, h: (col + h, n, 0))

    def prev(col):
        return pl.BlockSpec(
            (None, prev_rows, HEAD_DIM),
            lambda n, h: (col + h, jnp.maximum(n * ratio - 1, 0), 0))

    out_spec = pl.BlockSpec((None, DIL_ROWS, HEAD_DIM), lambda n, h: (h, n, 0))
    return pl.pallas_call(
        functools.partial(_dilated_kernel, dilation=dilation,
                          chains_per_stage=DIL_STAGE_CHAINS[dilation]),
        grid=(s // DIL_ROWS, HEADS_PER_GROUP),
        in_specs=[cur(q_col), cur(k_col), cur(v_col), prev(k_col),
                  prev(v_col)],
        out_specs=[out_spec, out_spec],
        out_shape=[jax.ShapeDtypeStruct((HEADS_PER_GROUP, s, HEAD_DIM), F32)] * 2,
        compiler_params=_params("parallel", "arbitrary"),
        name=f"dilated_d{dilation}",
    )(proj, proj, proj, proj, proj)


def _mem_attn(queries, mkv_ref):
    scale = HEAD_DIM ** -0.5
    ss = [_dot_nt(q.astype(BF16),
                  mkv_ref[:, hh * HEAD_DIM:(hh + 1) * HEAD_DIM]) * scale
          for hh, q in enumerate(queries)]
    es = [jnp.exp(s - jnp.max(s, axis=-1, keepdims=True)) for s in ss]
    ps = [(e / jnp.sum(e, axis=-1, keepdims=True)).astype(BF16) for e in es]
    return jnp.concatenate(
        [_dot(p, mkv_ref[:, MEM_W + hh * HEAD_DIM:
                         MEM_W + (hh + 1) * HEAD_DIM]).astype(BF16)
         for hh, p in enumerate(ps)], axis=1)


def _project_out(attn, o_mem, x_ref, w_ref, g_ref, out_ref):
    split = attn.shape[1]
    y = _dot(attn, w_ref[0:split, :]) + _dot(o_mem, w_ref[split:, :])
    out_ref[...] = x_ref[...] + _rms(y, g_ref[...])


def _outproj_merge_kernel(o1, o2, o3, l1, l2, l3, qm_ref, mkv_ref, x_ref, w_ref,
                          g_ref, out_ref):
    merged = []
    for hh in range(HEADS_PER_GROUP):
        a1, a2, a3 = l1[hh], l2[hh], l3[hh]
        mx = jnp.maximum(jnp.maximum(a1, a2), a3)
        e1, e2, e3 = jnp.exp(a1 - mx), jnp.exp(a2 - mx), jnp.exp(a3 - mx)
        den = e1 + e2 + e3
        o_dil = (e1 / den) * o1[hh] + (e2 / den) * o2[hh] + (e3 / den) * o3[hh]
        merged.append(o_dil.astype(BF16))
    o_mem = _mem_attn([qm_ref[hh] for hh in range(MEM_HEADS)], mkv_ref)
    _project_out(jnp.concatenate(merged, axis=1), o_mem, x_ref, w_ref, g_ref,
                 out_ref)


def _outproj_merge(outs, lses, proj, mkv, x, w_o, g, *, tm):
    s, d = x.shape
    row = lambda w: pl.BlockSpec((tm, w), lambda i: (i, 0))
    full = lambda a: pl.BlockSpec(a.shape, lambda i: (0, 0))
    heads = lambda n, blk: pl.BlockSpec((n, tm, HEAD_DIM),
                                        lambda i: (blk, i, 0))
    qm_block = proj.shape[0] // MEM_HEADS - 1
    g2 = g.reshape(1, d)
    return pl.pallas_call(
        _outproj_merge_kernel,
        grid=(s // tm,),
        in_specs=[heads(HEADS_PER_GROUP, 0)] * 6
        + [heads(MEM_HEADS, qm_block), full(mkv), row(d), full(w_o), full(g2)],
        out_specs=row(d),
        out_shape=jax.ShapeDtypeStruct((s, d), F32),
        compiler_params=_params("parallel"),
        name="outproj_merge",
    )(*outs, *lses, proj, mkv, x, w_o, g2)


def _outproj_kernel(oa_ref, qm_ref, mkv_ref, x_ref, w_ref, g_ref, out_ref):
    queries = [qm_ref[:, hh * HEAD_DIM:(hh + 1) * HEAD_DIM]
               for hh in range(MEM_HEADS)]
    _project_out(oa_ref[...], _mem_attn(queries, mkv_ref), x_ref, w_ref, g_ref,
                 out_ref)


def _outproj(o_attn, proj, qm_tile, mkv, x, w_o, g, *, tm):
    s, d = x.shape
    row = lambda w: pl.BlockSpec((tm, w), lambda i: (i, 0))
    full = lambda a: pl.BlockSpec(a.shape, lambda i: (0, 0))
    g2 = g.reshape(1, d)
    return pl.pallas_call(
        _outproj_kernel,
        grid=(s // tm,),
        in_specs=[row(o_attn.shape[1]),
                  pl.BlockSpec((tm, MEM_W), lambda i: (i, qm_tile)), full(mkv),
                  row(d), full(w_o), full(g2)],
        out_specs=row(d),
        out_shape=jax.ShapeDtypeStruct((s, d), F32),
        compiler_params=_params("parallel"),
        name="outproj",
    )(o_attn, proj, mkv, x, w_o, g2)


def _ffn_kernel(x_ref, xh_ref, gin_ref, wg_ref, wv_ref, wc_ref, bc_ref, wd_ref,
                gout_ref, o_ref, hn_ref, hh_ref, gbuf_ref, act_ref, *, tm, nf):
    i = pl.program_id(0)
    j = pl.program_id(1)
    halo = SUBLANE_BF16

    def up(slot):
        wg = wg_ref[...]
        gate = _dot(hn_ref[0], wg)
        val = _dot(hn_ref[0], wv_ref[...])
        gate_halo = jnp.where(i > 0, _dot(hh_ref[...], wg), 0.0)
        gbuf_ref[0:halo, :] = gate_halo
        gbuf_ref[halo:halo + tm, :] = gate
        g_m1 = gbuf_ref[pl.ds(halo - 1, tm), :]
        g_m2 = gbuf_ref[pl.ds(halo - 2, tm), :]
        acc = bc_ref[...] + gate * wc_ref[2:3, :]
        acc = acc + g_m2 * wc_ref[0:1, :]
        acc = acc + g_m1 * wc_ref[1:2, :]
        act = (acc * (1.0 / (1.0 + jnp.exp(-acc)))) * val
        act_ref[slot] = act.astype(BF16)

    def down(slot):
        o_ref[...] += _dot(act_ref[slot], wd_ref[...])

    @pl.when(j == 0)
    def _():
        _norm_rows(x_ref, gin_ref, hn_ref)
        hh_ref[...] = _rms(xh_ref[...], gin_ref[...]).astype(BF16)
        o_ref[...] = jnp.zeros_like(o_ref)
        up(0)

    @pl.when((j > 0) & (j < nf))
    def _():
        slot = lax.rem(j, 2)
        down(1 - slot)
        up(slot)

    @pl.when(j == nf)
    def _():
        down((nf - 1) % 2)
        o_ref[...] = x_ref[...] + _rms(o_ref[...], gout_ref[...])


def _ffn(x, g_in, w_up, w_conv, b_conv, w_down, g_out, layer, *, tm, tf):
    s, d = x.shape
    depth, d_ff, _ = w_down.shape
    assert s % tm == 0 and d_ff % tf == 0 and tm % SUBLANE_BF16 == 0
    nf = d_ff // tf
    halo = SUBLANE_BF16
    up_j = lambda j: jnp.minimum(j, nf - 1)
    down_j = lambda j: jnp.maximum(j - 1, 0)
    return pl.pallas_call(
        functools.partial(_ffn_kernel, tm=tm, nf=nf),
        grid=(s // tm, nf + 1),
        in_specs=[
            pl.BlockSpec((tm, d), lambda i, j: (i, 0)),
            pl.BlockSpec((halo, d),
                         lambda i, j: (jnp.maximum(i * (tm // halo) - 1, 0), 0)),
            pl.BlockSpec((1, d), lambda i, j: (0, 0)),
            pl.BlockSpec((None, d, tf), lambda i, j: (layer, 0, up_j(j))),
            pl.BlockSpec((None, d, tf), lambda i, j: (layer, 0, nf + up_j(j))),
            pl.BlockSpec((None, CONV_WIDTH, tf),
                         lambda i, j: (layer, 0, up_j(j))),
            pl.BlockSpec((None, 1, tf), lambda i, j: (layer, 0, up_j(j))),
            pl.BlockSpec((None, tf, d), lambda i, j: (layer, down_j(j), 0)),
            pl.BlockSpec((1, d), lambda i, j: (0, 0)),
        ],
        out_specs=pl.BlockSpec((tm, d), lambda i, j: (i, 0)),
        out_shape=jax.ShapeDtypeStruct((s, d), F32),
        scratch_shapes=[pltpu.VMEM((1, tm, d), BF16),
                        pltpu.VMEM((halo, d), BF16),
                        pltpu.VMEM((halo + tm, tf), F32),
                        pltpu.VMEM((2, tm, tf), BF16)],
        compiler_params=_params("parallel", "arbitrary"),
        name="conv_ffn",
    )(x, x, g_in.reshape(1, d), w_up, w_up, w_conv,
      b_conv.reshape(depth, 1, d_ff), w_down, g_out.reshape(1, d))


def _sb_kernel(q_ref, k_ref, v_ref, o_ref, acc_ref, carry_ref, tri_ref, *, tq,
               heads):
    i = pl.program_id(1)
    scale = HEAD_DIM ** -0.5
    half = tq // 2
    hsl = [slice(h * HEAD_DIM, (h + 1) * HEAD_DIM) for h in range(heads)]

    @pl.when(i == 0)
    def _():
        rr = lax.broadcasted_iota(jnp.int32, (2 * half, 2 * half), 0)
        cc = lax.broadcasted_iota(jnp.int32, (2 * half, 2 * half), 1)
        tri_ref[...] = jnp.where((cc >= half) | ((rr & (half - 1)) > cc),
                                 1.0, 0.0).astype(BF16)

    tri = tri_ref[...]
    qi = lax.broadcasted_iota(jnp.int32, (half, half), 0)
    kj = lax.broadcasted_iota(jnp.int32, (half, half), 1)
    strictly_earlier = kj < qi

    def run(segments, nrows, nk, diagonal_last):
        chains = [(slice(r0, r0 + nrows), pl.ds(ks, nk), s)
                  for r0, ks in segments for s in hsl]
        n_half = nk // half
        last = (n_half - 1) * half

        def mask_last(t):
            if not diagonal_last:
                return t
            tail = jnp.where(strictly_earlier, t[:, last:], 0.0)
            return tail if n_half == 1 else jnp.concatenate(
                [t[:, :last], tail], axis=1)

        zs = [_dot_nt(q_ref[r, s], k_ref[k, s]) * scale for r, k, s in chains]
        softs = [jnp.log(1.0 + jnp.exp2(jnp.abs(z) * NEG_LOG2E)) for z in zs]
        log_sigs = [jnp.minimum(z, 0.0) - t for z, t in zip(zs, softs)]
        drops = [mask_last(jnp.maximum(z, 0.0) + t) for z, t in zip(zs, softs)]
        his = [d.astype(BF16) for d in drops]
        los = [(d - hi.astype(F32)).astype(BF16) for d, hi in zip(drops, his)]
        first = diagonal_last
        carries = [None if first else carry_ref[r, s] for r, _, s in chains]
        parts = [[None] * n_half for _ in chains]
        for c in reversed(range(n_half)):
            sl = slice(c * half, (c + 1) * half)
            ress = [_dot(jnp.concatenate([hi[:, sl], lo[:, sl]], axis=1), tri)
                    for hi, lo in zip(his, los)]
            for n, res in enumerate(ress):
                if carries[n] is None:
                    parts[n][c] = jnp.exp(log_sigs[n][:, sl] - res[:, :half])
                    carries[n] = res[:, half:]
                else:
                    parts[n][c] = jnp.exp(log_sigs[n][:, sl]
                                          - (carries[n] + res[:, :half]))
                    carries[n] = carries[n] + res[:, half:]
        for n, (r, k, s) in enumerate(chains):
            a = mask_last(jnp.concatenate(parts[n], axis=1))
            pv = _dot(a.astype(BF16), v_ref[k, s])
            if first:
                acc_ref[r, s] = pv
            else:
                acc_ref[r, s] += pv
            carry_ref[r, s] = carries[n]

    @pl.when(i == 0)
    def _():
        run([(0, 0)], half, half, True)
        run([(half, 0)], half, tq, True)

    @pl.when(i > 0)
    def _():
        k0 = pl.multiple_of((i - 1) * tq, tq)
        run([(0, k0), (half, k0 + half)], half, tq + half, True)

    def live():
        return jnp.min(carry_ref[...]) < -EXP_UNDERFLOW_F32

    alive = live()

    @pl.when((i > 0) & alive)
    def _():
        run([(half, pl.multiple_of((i - 1) * tq, tq))], half, half, False)

    def cond(c):
        step, alive = c
        return (step < i) & alive

    def body(c):
        step, _ = c
        run([(0, pl.multiple_of((i - 1 - step) * tq, tq))], tq, tq, False)
        return step + 1, live()

    lax.while_loop(cond, body, (jnp.int32(1), alive))
    o_ref[...] = acc_ref[...].astype(o_ref.dtype)


def _stick_breaking(kvq, *, tq, heads):
    s = kvq.shape[0]
    assert tq == 2 * HEAD_DIM and s % tq == 0 and SB_HEADS % heads == 0
    groups = SB_HEADS // heads
    width = heads * HEAD_DIM
    whole = functools.partial(pl.BlockSpec, (s, width),
                              pipeline_mode=pl.Buffered(1))
    return pl.pallas_call(
        functools.partial(_sb_kernel, tq=tq, heads=heads),
        grid=(groups, s // tq),
        in_specs=[pl.BlockSpec((tq, width), lambda g, i: (i, 2 * groups + g)),
                  whole(lambda g, i: (0, g)),
                  whole(lambda g, i: (0, groups + g))],
        out_specs=pl.BlockSpec((tq, width), lambda g, i: (i, g)),
        out_shape=jax.ShapeDtypeStruct((s, SB_W), BF16),
        scratch_shapes=[pltpu.VMEM((tq, width), F32),
                        pltpu.VMEM((tq, width), F32),
                        pltpu.VMEM((tq, tq), BF16)],
        compiler_params=_params("arbitrary", "arbitrary"),
        name="stick_breaking",
    )(kvq, kvq, kvq)


def _rope_tables(s):
    pos = jnp.arange(s, dtype=F32)
    inv = ROPE_THETA ** (-jnp.arange(0, HEAD_DIM, 2, dtype=F32) / HEAD_DIM)
    ang = pos[:, None] * inv[None, :]
    cos, sin = jnp.cos(ang), jnp.sin(ang)
    return (jnp.concatenate([cos, cos], axis=-1),
            jnp.concatenate([-sin, sin], axis=-1))


def kernel(x, mem, norms, w_in_a, w_o_a, g_kv, w_kv, w_in_b, w_o_b, w_mem_kv,
           w_up, w_conv, b_conv, w_down):
    b, s, d = x.shape
    assert b == 1 and norms.shape[0] == 2
    xs = x.reshape(s, d)
    mems = mem.reshape(mem.shape[1], d)
    bf = lambda a: a.astype(BF16)
    tm = min(1024, s)

    nrm = norms[0]
    cos2, sin2 = _rope_tables(s)
    proj = _norm_matmul(xs, nrm[0:1], bf(w_in_a[0]), tm=tm, tn=2 * GROUP_W,
                        out_dtype=F32, rope_cols=2 * DIL_W, cos2=cos2,
                        sin2=sin2)
    mkv = _norm_matmul(mems, nrm[4:5], bf(w_mem_kv[0]), tm=mems.shape[0],
                       tn=GROUP_W, out_dtype=BF16)
    outs, lses = [], []
    for gi, (_, dilation) in enumerate(DIL_GROUPS):
        o_g, l_g = _dilated_group(proj, gi, dilation)
        outs.append(o_g)
        lses.append(l_g)
    xs = _outproj_merge(outs, lses, proj, mkv, xs, bf(w_o_a[0]), nrm[1],
                        tm=min(512, s))
    w_up_bf, w_down_bf = bf(w_up), bf(w_down)
    xs = _ffn(xs, nrm[2], w_up_bf, w_conv, b_conv, w_down_bf, nrm[3], 0,
              tm=min(1024, s), tf=512)

    nrm = norms[1]
    kvq = _norm_matmul(xs, jnp.stack([g_kv, nrm[0]]),
                       bf(jnp.concatenate([w_kv, w_in_b[0]], axis=1)),
                       tm=tm, tn=2 * GROUP_W, out_dtype=BF16,
                       gain_split_col=2 * SB_W)
    mkv = _norm_matmul(mems, nrm[4:5], bf(w_mem_kv[1]), tm=mems.shape[0],
                       tn=GROUP_W, out_dtype=BF16)
    o_sb = _stick_breaking(kvq, tq=2 * HEAD_DIM, heads=4)
    xs = _outproj(o_sb, kvq, 3 * SB_W // MEM_W, mkv, xs, bf(w_o_b[0]), nrm[1],
                  tm=min(512, s))
    xs = _ffn(xs, nrm[2], w_up_bf, w_conv, b_conv, w_down_bf, nrm[3], 1,
              tm=min(1024, s), tf=512)
    return xs.reshape(b, s, d)
```

```python
import functools

import jax
import jax.numpy as jnp
from jax import lax
from jax.experimental import pallas as pl
from jax.experimental.pallas import tpu as pltpu

HEAD_DIM = 128
N_MEM = 256
MEM_HEADS = 4
DIL_GROUPS = ((128, 1), (512, 4), (2048, 16))
HEADS_PER_GROUP = 4
DIL_HEADS = len(DIL_GROUPS) * HEADS_PER_GROUP
SB_HEADS = 12
BLOCK = 128
CONV_WIDTH = 3
ROPE_THETA = 10000.0
EPS = 1e-6
NEG_INF = -1e30
EXP_UNDERFLOW_F32 = -105.0
NEG_LOG2E = -1.4426950408889634

DIL_W = DIL_HEADS * HEAD_DIM
MEM_W = MEM_HEADS * HEAD_DIM
SB_W = SB_HEADS * HEAD_DIM
GROUP_W = HEADS_PER_GROUP * HEAD_DIM
DIL_ROWS = BLOCK * max(d for _, d in DIL_GROUPS)
DIL_STAGE_CHAINS = {1: 16, 4: 16, 16: 8}
SB_SEGMENT = 128
SB_HISTORY = 256

F32 = jnp.float32
BF16 = jnp.bfloat16

VMEM_LIMIT_BYTES = 59 * 1024 * 1024
SUBLANE_BF16 = 16


def _params(*sem):
    return pltpu.CompilerParams(dimension_semantics=sem,
                                vmem_limit_bytes=VMEM_LIMIT_BYTES)


def _rms(xf, g):
    y = xf * lax.rsqrt(jnp.mean(xf * xf, axis=-1, keepdims=True) + EPS)
    return y * g


def _dot(a, b):
    return jnp.dot(a, b, preferred_element_type=F32)


def _dot_nt(a, b):
    return lax.dot_general(a, b, (((1,), (1,)), ((), ())),
                           preferred_element_type=F32)


def _norm_rows(x_ref, g_ref, h_ref):
    xf = x_ref[...]
    y = xf * lax.rsqrt(jnp.mean(xf * xf, axis=-1, keepdims=True) + EPS)
    for gi in range(h_ref.shape[0]):
        h_ref[gi] = (y * g_ref[gi:gi + 1, :]).astype(BF16)


def _norm_matmul_kernel(*refs, rope_tiles, heads_per_tile, gain_split):
    if rope_tiles:
        x_ref, g_ref, w_ref, cos_ref, sin_ref, o_ref, h_ref = refs
    else:
        x_ref, g_ref, w_ref, o_ref, h_ref = refs
    j = pl.program_id(1)
    n_gains = h_ref.shape[0]

    @pl.when(j == 0)
    def _():
        _norm_rows(x_ref, g_ref, h_ref)

    if n_gains == 1:
        lhs = h_ref[0]
    else:
        lhs = h_ref[jnp.where(j >= gain_split, 1, 0)]
    acc = _dot(lhs, w_ref[...])

    if rope_tiles:
        is_rope = j < rope_tiles
        c = cos_ref[...]
        s = sin_ref[...]
        for hh in range(heads_per_tile):
            seg = acc[:, hh * HEAD_DIM:(hh + 1) * HEAD_DIM]
            rot = pltpu.roll(seg, HEAD_DIM // 2, 1)
            o_ref[hh] = jnp.where(is_rope, seg * c + rot * s,
                                  seg).astype(o_ref.dtype)
    else:
        o_ref[...] = acc.astype(o_ref.dtype)


def _norm_matmul(x, gains, w, *, tm, tn, out_dtype, gain_split_col=0,
                 rope_cols=0, cos2=None, sin2=None):
    m, k = x.shape
    n = w.shape[1]
    n_gains = gains.shape[0]
    assert m % tm == 0 and n % tn == 0
    assert rope_cols % tn == 0 and gain_split_col % tn == 0
    in_specs = [
        pl.BlockSpec((tm, k), lambda i, j: (i, 0)),
        pl.BlockSpec((n_gains, k), lambda i, j: (0, 0)),
        pl.BlockSpec((k, tn), lambda i, j: (0, j)),
    ]
    args = [x, gains, w]
    if rope_cols:
        in_specs += [pl.BlockSpec((tm, HEAD_DIM), lambda i, j: (i, 0))] * 2
        args += [cos2, sin2]
        out_spec = pl.BlockSpec((tn // HEAD_DIM, tm, HEAD_DIM),
                                lambda i, j: (j, i, 0))
        out_shape = jax.ShapeDtypeStruct((n // HEAD_DIM, m, HEAD_DIM), out_dtype)
    else:
        out_spec = pl.BlockSpec((tm, tn), lambda i, j: (i, j))
        out_shape = jax.ShapeDtypeStruct((m, n), out_dtype)
    return pl.pallas_call(
        functools.partial(_norm_matmul_kernel, rope_tiles=rope_cols // tn,
                          heads_per_tile=tn // HEAD_DIM,
                          gain_split=gain_split_col // tn),
        grid=(m // tm, n // tn),
        in_specs=in_specs,
        out_specs=out_spec,
        out_shape=out_shape,
        scratch_shapes=[pltpu.VMEM((n_gains, tm, k), BF16)],
        compiler_params=_params("parallel", "arbitrary"),
        name="norm_matmul",
    )(*args)


def _dilated_kernel(q_ref, kc_ref, vc_ref, kp_ref, vp_ref, o_ref, l_ref, *,
                    dilation, chains_per_stage):
    n = pl.program_id(0)
    scale = HEAD_DIM ** -0.5
    nq = DIL_ROWS // (BLOCK * dilation)
    qi = lax.broadcasted_iota(jnp.int32, (BLOCK, BLOCK), 0)
    kj = lax.broadcasted_iota(jnp.int32, (BLOCK, BLOCK), 1)
    band = kj >= qi
    valid_cur = kj <= qi

    def rows(ref, b, r):
        start = b * BLOCK * dilation + r
        if dilation == 1:
            return pl.ds(start, BLOCK)
        return pl.ds(start, BLOCK, stride=dilation)

    loaded = {}

    def kv(b, r):
        if (b, r) not in loaded:
            if b < 0:
                idx = rows(kp_ref, 0, r)
                loaded[(b, r)] = (kp_ref[idx, :].astype(BF16),
                                  vp_ref[idx, :].astype(BF16))
            else:
                idx = rows(kc_ref, b, r)
                loaded[(b, r)] = (kc_ref[idx, :].astype(BF16),
                                  vc_ref[idx, :].astype(BF16))
        return loaded[(b, r)]

    def run_stage(chains):
        qs, kps, kcs, vps, vcs, masks, idxs = [], [], [], [], [], [], []
        for b, r in chains:
            idx = rows(q_ref, b, r)
            qs.append(q_ref[idx, :].astype(BF16))
            kp, vp = kv(b - 1, r)
            kc, vc = kv(b, r)
            kps.append(kp); vps.append(vp); kcs.append(kc); vcs.append(vc)
            masks.append(band if b > 0 else band & (n > 0))
            idxs.append(idx)
        sp = [jnp.where(m, _dot_nt(q, k) * scale, NEG_INF)
              for q, k, m in zip(qs, kps, masks)]
        sc = [jnp.where(valid_cur, _dot_nt(q, k) * scale, NEG_INF)
              for q, k in zip(qs, kcs)]
        mx = [jnp.maximum(jnp.max(a, axis=-1, keepdims=True),
                          jnp.max(c, axis=-1, keepdims=True))
              for a, c in zip(sp, sc)]
        tot = [jnp.sum(jnp.exp(a - m), axis=-1, keepdims=True)
               + jnp.sum(jnp.exp(c - m), axis=-1, keepdims=True)
               for a, c, m in zip(sp, sc, mx)]
        lse = [m + jnp.log(t) for m, t in zip(mx, tot)]
        pp = [jnp.exp(a - l).astype(BF16) for a, l in zip(sp, lse)]
        pc = [jnp.exp(c - l).astype(BF16) for c, l in zip(sc, lse)]
        for idx, a, c, vp, vc, l in zip(idxs, pp, pc, vps, vcs, lse):
            o_ref[idx, :] = _dot(a, vp) + _dot(c, vc)
            l_ref[idx, :] = jnp.broadcast_to(l, (BLOCK, HEAD_DIM))

    chains = [(b, r) for b in range(nq) for r in range(dilation)]
    for c0 in range(0, len(chains), chains_per_stage):
        run_stage(chains[c0:c0 + chains_per_stage])


def _dilated_group(proj, gi, dilation):
    s = proj.shape[1]
    assert s % DIL_ROWS == 0 and DIL_ROWS % (BLOCK * dilation) == 0
    prev_rows = BLOCK * dilation
    ratio = DIL_ROWS // prev_rows
    q_col = gi * HEADS_PER_GROUP
    k_col = DIL_HEADS + q_col
    v_col = 2 * DIL_HEADS + q_col

    def cur(col):
        return pl.BlockSpec((None, DIL_ROWS, HEAD_DIM),
                            lambda n, h: (col + h, n, 0))

    def prev(col):
        return pl.BlockSpec(
            (None, prev_rows, HEAD_DIM),
            lambda n, h: (col + h, jnp.maximum(n * ratio - 1, 0), 0))

    out_spec = pl.BlockSpec((None, DIL_ROWS, HEAD_DIM), lambda n, h: (h, n, 0))
    return pl.pallas_call(
        functools.partial(_dilated_kernel, dilation=dilation,
                          chains_per_stage=DIL_STAGE_CHAINS[dilation]),
        grid=(s // DIL_ROWS, HEADS_PER_GROUP),
        in_specs=[cur(q_col), cur(k_col), cur(v_col), prev(k_col),
                  prev(v_col)],
        out_specs=[out_spec, out_spec],
        out_shape=[jax.ShapeDtypeStruct((HEADS_PER_GROUP, s, HEAD_DIM), F32)] * 2,
        compiler_params=_params("parallel", "arbitrary"),
        name=f"dilated_d{dilation}",
    )(proj, proj, proj, proj, proj)


def _mem_attn(queries, mkv_ref):
    scale = HEAD_DIM ** -0.5
    ss = [_dot_nt(q.astype(BF16),
                  mkv_ref[:, hh * HEAD_DIM:(hh + 1) * HEAD_DIM]) * scale
          for hh, q in enumerate(queries)]
    es = [jnp.exp(s - jnp.max(s, axis=-1, keepdims=True)) for s in ss]
    ps = [(e / jnp.sum(e, axis=-1, keepdims=True)).astype(BF16) for e in es]
    return jnp.concatenate(
        [_dot(p, mkv_ref[:, MEM_W + hh * HEAD_DIM:
                         MEM_W + (hh + 1) * HEAD_DIM]).astype(BF16)
         for hh, p in enumerate(ps)], axis=1)


def _project_out(attn, o_mem, x_ref, w_ref, g_ref, out_ref):
    split = attn.shape[1]
    y = _dot(attn, w_ref[0:split, :]) + _dot(o_mem, w_ref[split:, :])
    out_ref[...] = x_ref[...] + _rms(y, g_ref[...])


def _outproj_merge_kernel(o1, o2, o3, l1, l2, l3, qm_ref, mkv_ref, x_ref, w_ref,
                          g_ref, out_ref):
    merged = []
    for hh in range(HEADS_PER_GROUP):
        a1, a2, a3 = l1[hh], l2[hh], l3[hh]
        mx = jnp.maximum(jnp.maximum(a1, a2), a3)
        e1, e2, e3 = jnp.exp(a1 - mx), jnp.exp(a2 - mx), jnp.exp(a3 - mx)
        den = e1 + e2 + e3
        o_dil = (e1 / den) * o1[hh] + (e2 / den) * o2[hh] + (e3 / den) * o3[hh]
        merged.append(o_dil.astype(BF16))
    o_mem = _mem_attn([qm_ref[hh] for hh in range(MEM_HEADS)], mkv_ref)
    _project_out(jnp.concatenate(merged, axis=1), o_mem, x_ref, w_ref, g_ref,
                 out_ref)


def _outproj_merge(outs, lses, proj, mkv, x, w_o, g, *, tm):
    s, d = x.shape
    row = lambda w: pl.BlockSpec((tm, w), lambda i: (i, 0))
    full = lambda a: pl.BlockSpec(a.shape, lambda i: (0, 0))
    heads = lambda n, blk: pl.BlockSpec((n, tm, HEAD_DIM),
                                        lambda i: (blk, i, 0))
    qm_block = proj.shape[0] // MEM_HEADS - 1
    g2 = g.reshape(1, d)
    return pl.pallas_call(
        _outproj_merge_kernel,
        grid=(s // tm,),
        in_specs=[heads(HEADS_PER_GROUP, 0)] * 6
        + [heads(MEM_HEADS, qm_block), full(mkv), row(d), full(w_o), full(g2)],
        out_specs=row(d),
        out_shape=jax.ShapeDtypeStruct((s, d), F32),
        compiler_params=_params("parallel"),
        name="outproj_merge",
    )(*outs, *lses, proj, mkv, x, w_o, g2)


def _outproj_kernel(oa_ref, qm_ref, mkv_ref, x_ref, w_ref, g_ref, out_ref):
    queries = [qm_ref[:, hh * HEAD_DIM:(hh + 1) * HEAD_DIM]
               for hh in range(MEM_HEADS)]
    _project_out(oa_ref[...], _mem_attn(queries, mkv_ref), x_ref, w_ref, g_ref,
                 out_ref)


def _outproj(o_attn, proj, qm_tile, mkv, x, w_o, g, *, tm):
    s, d = x.shape
    row = lambda w: pl.BlockSpec((tm, w), lambda i: (i, 0))
    full = lambda a: pl.BlockSpec(a.shape, lambda i: (0, 0))
    g2 = g.reshape(1, d)
    return pl.pallas_call(
        _outproj_kernel,
        grid=(s // tm,),
        in_specs=[row(o_attn.shape[1]),
                  pl.BlockSpec((tm, MEM_W), lambda i: (i, qm_tile)), full(mkv),
                  row(d), full(w_o), full(g2)],
        out_specs=row(d),
        out_shape=jax.ShapeDtypeStruct((s, d), F32),
        compiler_params=_params("parallel"),
        name="outproj",
    )(o_attn, proj, mkv, x, w_o, g2)


def _ffn_kernel(x_ref, xh_ref, gin_ref, wg_ref, wv_ref, wc_ref, bc_ref, wd_ref,
                gout_ref, o_ref, hn_ref, hh_ref, gbuf_ref, act_ref, *, tm, nf):
    i = pl.program_id(0)
    j = pl.program_id(1)
    halo = SUBLANE_BF16

    def up(slot):
        wg = wg_ref[...]
        gate = _dot(hn_ref[0], wg)
        val = _dot(hn_ref[0], wv_ref[...])
        gate_halo = jnp.where(i > 0, _dot(hh_ref[...], wg), 0.0)
        gbuf_ref[0:halo, :] = gate_halo
        gbuf_ref[halo:halo + tm, :] = gate
        g_m1 = gbuf_ref[pl.ds(halo - 1, tm), :]
        g_m2 = gbuf_ref[pl.ds(halo - 2, tm), :]
        acc = bc_ref[...] + gate * wc_ref[2:3, :]
        acc = acc + g_m2 * wc_ref[0:1, :]
        acc = acc + g_m1 * wc_ref[1:2, :]
        act = (acc * (1.0 / (1.0 + jnp.exp(-acc)))) * val
        act_ref[slot] = act.astype(BF16)

    def down(slot):
        o_ref[...] += _dot(act_ref[slot], wd_ref[...])

    @pl.when(j == 0)
    def _():
        _norm_rows(x_ref, gin_ref, hn_ref)
        hh_ref[...] = _rms(xh_ref[...], gin_ref[...]).astype(BF16)
        o_ref[...] = jnp.zeros_like(o_ref)
        up(0)

    @pl.when((j > 0) & (j < nf))
    def _():
        slot = lax.rem(j, 2)
        down(1 - slot)
        up(slot)

    @pl.when(j == nf)
    def _():
        down((nf - 1) % 2)
        o_ref[...] = x_ref[...] + _rms(o_ref[...], gout_ref[...])


def _ffn(x, g_in, w_up, w_conv, b_conv, w_down, g_out, layer, *, tm, tf):
    s, d = x.shape
    depth, d_ff, _ = w_down.shape
    assert s % tm == 0 and d_ff % tf == 0 and tm % SUBLANE_BF16 == 0
    nf = d_ff // tf
    halo = SUBLANE_BF16
    up_j = lambda j: jnp.minimum(j, nf - 1)
    down_j = lambda j: jnp.maximum(j - 1, 0)
    return pl.pallas_call(
        functools.partial(_ffn_kernel, tm=tm, nf=nf),
        grid=(s // tm, nf + 1),
        in_specs=[
            pl.BlockSpec((tm, d), lambda i, j: (i, 0)),
            pl.BlockSpec((halo, d),
                         lambda i, j: (jnp.maximum(i * (tm // halo) - 1, 0), 0)),
            pl.BlockSpec((1, d), lambda i, j: (0, 0)),
            pl.BlockSpec((None, d, tf), lambda i, j: (layer, 0, up_j(j))),
            pl.BlockSpec((None, d, tf), lambda i, j: (layer, 0, nf + up_j(j))),
            pl.BlockSpec((None, CONV_WIDTH, tf),
                         lambda i, j: (layer, 0, up_j(j))),
            pl.BlockSpec((None, 1, tf), lambda i, j: (layer, 0, up_j(j))),
            pl.BlockSpec((None, tf, d), lambda i, j: (layer, down_j(j), 0)),
            pl.BlockSpec((1, d), lambda i, j: (0, 0)),
        ],
        out_specs=pl.BlockSpec((tm, d), lambda i, j: (i, 0)),
        out_shape=jax.ShapeDtypeStruct((s, d), F32),
        scratch_shapes=[pltpu.VMEM((1, tm, d), BF16),
                        pltpu.VMEM((halo, d), BF16),
                        pltpu.VMEM((halo + tm, tf), F32),
                        pltpu.VMEM((2, tm, tf), BF16)],
        compiler_params=_params("parallel", "arbitrary"),
        name="conv_ffn",
    )(x, x, g_in.reshape(1, d), w_up, w_up, w_conv,
      b_conv.reshape(depth, 1, d_ff), w_down, g_out.reshape(1, d))


def _sb_kernel(q_ref, k_ref, v_ref, o_ref, acc_ref, carry_ref, tri_ref, *, tq,
               heads):
    i = pl.program_id(1)
    scale = HEAD_DIM ** -0.5
    half = SB_SEGMENT
    n_seg = tq // half
    hsl = [slice(h * HEAD_DIM, (h + 1) * HEAD_DIM) for h in range(heads)]

    @pl.when(i == 0)
    def _():
        rr = lax.broadcasted_iota(jnp.int32, (2 * half, 2 * half), 0)
        cc = lax.broadcasted_iota(jnp.int32, (2 * half, 2 * half), 1)
        tri_ref[...] = jnp.where((cc >= half) | ((rr & (half - 1)) > cc),
                                 1.0, 0.0).astype(BF16)

    tri = tri_ref[...]
    qi = lax.broadcasted_iota(jnp.int32, (half, half), 0)
    kj = lax.broadcasted_iota(jnp.int32, (half, half), 1)
    strictly_earlier = kj < qi

    def run(segments, nrows, nk, diagonal_last):
        chains = [(slice(r0, r0 + nrows), pl.ds(ks, nk), s)
                  for r0, ks in segments for s in hsl]
        n_half = nk // half
        last = (n_half - 1) * half

        def mask_last(t):
            if not diagonal_last:
                return t
            tail = jnp.where(strictly_earlier, t[:, last:], 0.0)
            return tail if n_half == 1 else jnp.concatenate(
                [t[:, :last], tail], axis=1)

        zs = [_dot_nt(q_ref[r, s], k_ref[k, s]) * scale for r, k, s in chains]
        softs = [jnp.log(1.0 + jnp.exp2(jnp.abs(z) * NEG_LOG2E)) for z in zs]
        log_sigs = [jnp.minimum(z, 0.0) - t for z, t in zip(zs, softs)]
        drops = [mask_last(jnp.maximum(z, 0.0) + t) for z, t in zip(zs, softs)]
        his = [d.astype(BF16) for d in drops]
        los = [(d - hi.astype(F32)).astype(BF16) for d, hi in zip(drops, his)]
        first = diagonal_last
        carries = [None if first else carry_ref[r, s] for r, _, s in chains]
        parts = [[None] * n_half for _ in chains]
        for c in reversed(range(n_half)):
            sl = slice(c * half, (c + 1) * half)
            ress = [_dot(jnp.concatenate([hi[:, sl], lo[:, sl]], axis=1), tri)
                    for hi, lo in zip(his, los)]
            for n, res in enumerate(ress):
                if carries[n] is None:
                    parts[n][c] = jnp.exp(log_sigs[n][:, sl] - res[:, :half])
                    carries[n] = res[:, half:]
                else:
                    parts[n][c] = jnp.exp(log_sigs[n][:, sl]
                                          - (carries[n] + res[:, :half]))
                    carries[n] = carries[n] + res[:, half:]
        for n, (r, k, s) in enumerate(chains):
            a = mask_last(jnp.concatenate(parts[n], axis=1))
            pv = _dot(a.astype(BF16), v_ref[k, s])
            if first:
                acc_ref[r, s] = pv
            else:
                acc_ref[r, s] += pv
            carry_ref[r, s] = carries[n]

    window = SB_HISTORY + half
    pairs = [list(range(s, min(s + 2, n_seg))) for s in range(0, n_seg, 2)]

    @pl.when(i == 0)
    def _():
        short = [s for s in range(n_seg) if s * half < SB_HISTORY]
        for s in short:
            run([(s * half, 0)], half, (s + 1) * half, True)
        rest = [(s * half, s * half - SB_HISTORY)
                for s in range(n_seg) if s not in short]
        if rest:
            run(rest, half, window, True)

    @pl.when(i > 0)
    def _():
        k0 = pl.multiple_of(i * tq - SB_HISTORY, half)
        for pair in pairs:
            run([(s * half, k0 + s * half) for s in pair], half, window, True)

    def live():
        return jnp.min(carry_ref[...]) < -EXP_UNDERFLOW_F32

    alive = live()

    @pl.when(alive & (i > 0))
    def _():
        k0 = pl.multiple_of(i * tq - SB_HISTORY, half)
        for s in range(1, n_seg):
            run([(s * half, k0)], half, s * half, False)

    @pl.when(alive & (i == 0))
    def _():
        for s in range(n_seg):
            unseen = s * half - SB_HISTORY
            if unseen > 0:
                run([(s * half, 0)], half, unseen, False)

    def cond(c):
        step, alive = c
        return (step < n_rest) & alive

    def body(c):
        step, _ = c
        kstart = pl.multiple_of(i * tq - SB_HISTORY * (step + 2), SB_HISTORY)
        run([(0, kstart)], tq, SB_HISTORY, False)
        return step + 1, live()

    n_rest = jnp.maximum((i * tq - SB_HISTORY) // SB_HISTORY, 0)
    lax.while_loop(cond, body, (jnp.int32(0), alive))
    o_ref[...] = acc_ref[...].astype(o_ref.dtype)


def _stick_breaking(kvq, *, tq, heads):
    s = kvq.shape[0]
    assert SB_SEGMENT == HEAD_DIM and SB_HISTORY % SB_SEGMENT == 0
    assert tq % SB_HISTORY == 0 and s % tq == 0 and SB_HEADS % heads == 0
    groups = SB_HEADS // heads
    width = heads * HEAD_DIM
    whole = functools.partial(pl.BlockSpec, (s, width),
                              pipeline_mode=pl.Buffered(1))
    return pl.pallas_call(
        functools.partial(_sb_kernel, tq=tq, heads=heads),
        grid=(groups, s // tq),
        in_specs=[pl.BlockSpec((tq, width), lambda g, i: (i, 2 * groups + g)),
                  whole(lambda g, i: (0, g)),
                  whole(lambda g, i: (0, groups + g))],
        out_specs=pl.BlockSpec((tq, width), lambda g, i: (i, g)),
        out_shape=jax.ShapeDtypeStruct((s, SB_W), BF16),
        scratch_shapes=[pltpu.VMEM((tq, width), F32),
                        pltpu.VMEM((tq, width), F32),
                        pltpu.VMEM((2 * SB_SEGMENT, 2 * SB_SEGMENT), BF16)],
        compiler_params=_params("arbitrary", "arbitrary"),
        name="stick_breaking",
    )(kvq, kvq, kvq)


def _rope_tables(s):
    pos = jnp.arange(s, dtype=F32)
    inv = ROPE_THETA ** (-jnp.arange(0, HEAD_DIM, 2, dtype=F32) / HEAD_DIM)
    ang = pos[:, None] * inv[None, :]
    cos, sin = jnp.cos(ang), jnp.sin(ang)
    return (jnp.concatenate([cos, cos], axis=-1),
            jnp.concatenate([-sin, sin], axis=-1))


def kernel(x, mem, norms, w_in_a, w_o_a, g_kv, w_kv, w_in_b, w_o_b, w_mem_kv,
           w_up, w_conv, b_conv, w_down):
    b, s, d = x.shape
    assert b == 1 and norms.shape[0] == 2
    xs = x.reshape(s, d)
    mems = mem.reshape(mem.shape[1], d)
    bf = lambda a: a.astype(BF16)
    tm = min(1024, s)

    nrm = norms[0]
    cos2, sin2 = _rope_tables(s)
    proj = _norm_matmul(xs, nrm[0:1], bf(w_in_a[0]), tm=tm, tn=2 * GROUP_W,
                        out_dtype=F32, rope_cols=2 * DIL_W, cos2=cos2,
                        sin2=sin2)
    mkv = _norm_matmul(mems, nrm[4:5], bf(w_mem_kv[0]), tm=mems.shape[0],
                       tn=GROUP_W, out_dtype=BF16)
    outs, lses = [], []
    for gi, (_, dilation) in enumerate(DIL_GROUPS):
        o_g, l_g = _dilated_group(proj, gi, dilation)
        outs.append(o_g)
        lses.append(l_g)
    xs = _outproj_merge(outs, lses, proj, mkv, xs, bf(w_o_a[0]), nrm[1],
                        tm=min(512, s))
    w_up_bf, w_down_bf = bf(w_up), bf(w_down)
    xs = _ffn(xs, nrm[2], w_up_bf, w_conv, b_conv, w_down_bf, nrm[3], 0,
              tm=min(1024, s), tf=512)

    nrm = norms[1]
    kvq = _norm_matmul(xs, jnp.stack([g_kv, nrm[0]]),
                       bf(jnp.concatenate([w_kv, w_in_b[0]], axis=1)),
                       tm=tm, tn=2 * GROUP_W, out_dtype=BF16,
                       gain_split_col=2 * SB_W)
    mkv = _norm_matmul(mems, nrm[4:5], bf(w_mem_kv[1]), tm=mems.shape[0],
                       tn=GROUP_W, out_dtype=BF16)
    o_sb = _stick_breaking(kvq, tq=min(512, s), heads=4)
    xs = _outproj(o_sb, kvq, 3 * SB_W // MEM_W, mkv, xs, bf(w_o_b[0]), nrm[1],
                  tm=min(512, s))
    xs = _ffn(xs, nrm[2], w_up_bf, w_conv, b_conv, w_down_bf, nrm[3], 1,
              tm=min(1024, s), tf=512)
    return xs.reshape(b, s, d)
```

```python
import functools

import jax
import jax.numpy as jnp
from jax import lax
from jax.experimental import pallas as pl
from jax.experimental.pallas import tpu as pltpu

HEAD_DIM = 128
N_MEM = 256
MEM_HEADS = 4
DIL_GROUPS = ((128, 1), (512, 4), (2048, 16))
HEADS_PER_GROUP = 4
DIL_HEADS = len(DIL_GROUPS) * HEADS_PER_GROUP
SB_HEADS = 12
BLOCK = 128
CONV_WIDTH = 3
ROPE_THETA = 10000.0
EPS = 1e-6
NEG_INF = -1e30
EXP_UNDERFLOW_F32 = -105.0
NEG_LOG2E = -1.4426950408889634

DIL_W = DIL_HEADS * HEAD_DIM
MEM_W = MEM_HEADS * HEAD_DIM
SB_W = SB_HEADS * HEAD_DIM
GROUP_W = HEADS_PER_GROUP * HEAD_DIM
DIL_ROWS = 2 * BLOCK * max(d for _, d in DIL_GROUPS)
DIL_STAGE_CHAINS = {1: 16, 4: 16, 16: 8}
SB_SEGMENT = 128
SB_HISTORY = 256

F32 = jnp.float32
BF16 = jnp.bfloat16

VMEM_LIMIT_BYTES = 59 * 1024 * 1024
SUBLANE_BF16 = 16


def _params(*sem):
    return pltpu.CompilerParams(dimension_semantics=sem,
                                vmem_limit_bytes=VMEM_LIMIT_BYTES)


def _rms(xf, g):
    y = xf * lax.rsqrt(jnp.mean(xf * xf, axis=-1, keepdims=True) + EPS)
    return y * g


def _dot(a, b):
    return jnp.dot(a, b, preferred_element_type=F32)


def _dot_nt(a, b):
    return lax.dot_general(a, b, (((1,), (1,)), ((), ())),
                           preferred_element_type=F32)


def _norm_rows(x_ref, g_ref, h_ref):
    xf = x_ref[...]
    y = xf * lax.rsqrt(jnp.mean(xf * xf, axis=-1, keepdims=True) + EPS)
    for gi in range(h_ref.shape[0]):
        h_ref[gi] = (y * g_ref[gi:gi + 1, :]).astype(BF16)


def _norm_matmul_kernel(*refs, rope_tiles, heads_per_tile, gain_split):
    if rope_tiles:
        x_ref, g_ref, w_ref, cos_ref, sin_ref, o_ref, h_ref = refs
    else:
        x_ref, g_ref, w_ref, o_ref, h_ref = refs
    j = pl.program_id(1)
    n_gains = h_ref.shape[0]

    @pl.when(j == 0)
    def _():
        _norm_rows(x_ref, g_ref, h_ref)

    if n_gains == 1:
        lhs = h_ref[0]
    else:
        lhs = h_ref[jnp.where(j >= gain_split, 1, 0)]
    acc = _dot(lhs, w_ref[...])

    if rope_tiles:
        is_rope = j < rope_tiles
        c = cos_ref[...]
        s = sin_ref[...]
        for hh in range(heads_per_tile):
            seg = acc[:, hh * HEAD_DIM:(hh + 1) * HEAD_DIM]
            rot = pltpu.roll(seg, HEAD_DIM // 2, 1)
            o_ref[hh] = jnp.where(is_rope, seg * c + rot * s,
                                  seg).astype(o_ref.dtype)
    else:
        o_ref[...] = acc.astype(o_ref.dtype)


def _norm_matmul(x, gains, w, *, tm, tn, out_dtype, gain_split_col=0,
                 rope_cols=0, cos2=None, sin2=None):
    m, k = x.shape
    n = w.shape[1]
    n_gains = gains.shape[0]
    assert m % tm == 0 and n % tn == 0
    assert rope_cols % tn == 0 and gain_split_col % tn == 0
    in_specs = [
        pl.BlockSpec((tm, k), lambda i, j: (i, 0)),
        pl.BlockSpec((n_gains, k), lambda i, j: (0, 0)),
        pl.BlockSpec((k, tn), lambda i, j: (0, j)),
    ]
    args = [x, gains, w]
    if rope_cols:
        in_specs += [pl.BlockSpec((tm, HEAD_DIM), lambda i, j: (i, 0))] * 2
        args += [cos2, sin2]
        out_spec = pl.BlockSpec((tn // HEAD_DIM, tm, HEAD_DIM),
                                lambda i, j: (j, i, 0))
        out_shape = jax.ShapeDtypeStruct((n // HEAD_DIM, m, HEAD_DIM), out_dtype)
    else:
        out_spec = pl.BlockSpec((tm, tn), lambda i, j: (i, j))
        out_shape = jax.ShapeDtypeStruct((m, n), out_dtype)
    return pl.pallas_call(
        functools.partial(_norm_matmul_kernel, rope_tiles=rope_cols // tn,
                          heads_per_tile=tn // HEAD_DIM,
                          gain_split=gain_split_col // tn),
        grid=(m // tm, n // tn),
        in_specs=in_specs,
        out_specs=out_spec,
        out_shape=out_shape,
        scratch_shapes=[pltpu.VMEM((n_gains, tm, k), BF16)],
        compiler_params=_params("parallel", "arbitrary"),
        name="norm_matmul",
    )(*args)


def _dilated_kernel(q_ref, kc_ref, vc_ref, kp_ref, vp_ref, o_ref, l_ref, *,
                    dilation, chains_per_stage):
    n = pl.program_id(0)
    scale = HEAD_DIM ** -0.5
    nq = DIL_ROWS // (BLOCK * dilation)
    qi = lax.broadcasted_iota(jnp.int32, (BLOCK, BLOCK), 0)
    kj = lax.broadcasted_iota(jnp.int32, (BLOCK, BLOCK), 1)
    band = kj >= qi
    valid_cur = kj <= qi

    def rows(ref, b, r):
        start = b * BLOCK * dilation + r
        if dilation == 1:
            return pl.ds(start, BLOCK)
        return pl.ds(start, BLOCK, stride=dilation)

    loaded = {}

    def kv(b, r):
        if (b, r) not in loaded:
            if b < 0:
                idx = rows(kp_ref, 0, r)
                loaded[(b, r)] = (kp_ref[idx, :].astype(BF16),
                                  vp_ref[idx, :].astype(BF16))
            else:
                idx = rows(kc_ref, b, r)
                loaded[(b, r)] = (kc_ref[idx, :].astype(BF16),
                                  vc_ref[idx, :].astype(BF16))
        return loaded[(b, r)]

    def run_stage(chains):
        qs, kps, kcs, vps, vcs, masks, idxs = [], [], [], [], [], [], []
        for b, r in chains:
            idx = rows(q_ref, b, r)
            qs.append(q_ref[idx, :].astype(BF16))
            kp, vp = kv(b - 1, r)
            kc, vc = kv(b, r)
            kps.append(kp); vps.append(vp); kcs.append(kc); vcs.append(vc)
            masks.append(band if b > 0 else band & (n > 0))
            idxs.append(idx)
        sp = [jnp.where(m, _dot_nt(q, k) * scale, NEG_INF)
              for q, k, m in zip(qs, kps, masks)]
        sc = [jnp.where(valid_cur, _dot_nt(q, k) * scale, NEG_INF)
              for q, k in zip(qs, kcs)]
        mx = [jnp.maximum(jnp.max(a, axis=-1, keepdims=True),
                          jnp.max(c, axis=-1, keepdims=True))
              for a, c in zip(sp, sc)]
        tot = [jnp.sum(jnp.exp(a - m), axis=-1, keepdims=True)
               + jnp.sum(jnp.exp(c - m), axis=-1, keepdims=True)
               for a, c, m in zip(sp, sc, mx)]
        lse = [m + jnp.log(t) for m, t in zip(mx, tot)]
        pp = [jnp.exp(a - l).astype(BF16) for a, l in zip(sp, lse)]
        pc = [jnp.exp(c - l).astype(BF16) for c, l in zip(sc, lse)]
        for idx, a, c, vp, vc, l in zip(idxs, pp, pc, vps, vcs, lse):
            o_ref[idx, :] = _dot(a, vp) + _dot(c, vc)
            l_ref[idx, :] = jnp.broadcast_to(l, (BLOCK, HEAD_DIM))

    chains = [(b, r) for b in range(nq) for r in range(dilation)]
    for c0 in range(0, len(chains), chains_per_stage):
        run_stage(chains[c0:c0 + chains_per_stage])


def _dilated_group(proj, gi, dilation):
    s = proj.shape[1]
    assert s % DIL_ROWS == 0 and DIL_ROWS % (BLOCK * dilation) == 0
    prev_rows = BLOCK * dilation
    ratio = DIL_ROWS // prev_rows
    q_col = gi * HEADS_PER_GROUP
    k_col = DIL_HEADS + q_col
    v_col = 2 * DIL_HEADS + q_col

    def cur(col):
        return pl.BlockSpec((None, DIL_ROWS, HEAD_DIM),
                            lambda n, h: (col + h, n, 0))

    def prev(col):
        return pl.BlockSpec(
            (None, prev_rows, HEAD_DIM),
            lambda n, h: (col + h, jnp.maximum(n * ratio - 1, 0), 0))

    out_spec = pl.BlockSpec((None, DIL_ROWS, HEAD_DIM), lambda n, h: (h, n, 0))
    return pl.pallas_call(
        functools.partial(_dilated_kernel, dilation=dilation,
                          chains_per_stage=DIL_STAGE_CHAINS[dilation]),
        grid=(s // DIL_ROWS, HEADS_PER_GROUP),
        in_specs=[cur(q_col), cur(k_col), cur(v_col), prev(k_col),
                  prev(v_col)],
        out_specs=[out_spec, out_spec],
        out_shape=[jax.ShapeDtypeStruct((HEADS_PER_GROUP, s, HEAD_DIM), F32)] * 2,
        compiler_params=_params("parallel", "arbitrary"),
        name=f"dilated_d{dilation}",
    )(proj, proj, proj, proj, proj)


def _mem_attn(queries, mkv_ref):
    scale = HEAD_DIM ** -0.5
    ss = [_dot_nt(q.astype(BF16),
                  mkv_ref[:, hh * HEAD_DIM:(hh + 1) * HEAD_DIM]) * scale
          for hh, q in enumerate(queries)]
    es = [jnp.exp(s - jnp.max(s, axis=-1, keepdims=True)) for s in ss]
    ps = [(e / jnp.sum(e, axis=-1, keepdims=True)).astype(BF16) for e in es]
    return jnp.concatenate(
        [_dot(p, mkv_ref[:, MEM_W + hh * HEAD_DIM:
                         MEM_W + (hh + 1) * HEAD_DIM]).astype(BF16)
         for hh, p in enumerate(ps)], axis=1)


def _project_out(attn, o_mem, x_ref, w_ref, g_ref, out_ref):
    split = attn.shape[1]
    y = _dot(attn, w_ref[0:split, :]) + _dot(o_mem, w_ref[split:, :])
    out_ref[...] = x_ref[...] + _rms(y, g_ref[...])


def _outproj_merge_kernel(o1, o2, o3, l1, l2, l3, qm_ref, mkv_ref, x_ref, w_ref,
                          g_ref, out_ref):
    merged = []
    for hh in range(HEADS_PER_GROUP):
        a1, a2, a3 = l1[hh], l2[hh], l3[hh]
        mx = jnp.maximum(jnp.maximum(a1, a2), a3)
        e1, e2, e3 = jnp.exp(a1 - mx), jnp.exp(a2 - mx), jnp.exp(a3 - mx)
        den = e1 + e2 + e3
        o_dil = (e1 / den) * o1[hh] + (e2 / den) * o2[hh] + (e3 / den) * o3[hh]
        merged.append(o_dil.astype(BF16))
    o_mem = _mem_attn([qm_ref[hh] for hh in range(MEM_HEADS)], mkv_ref)
    _project_out(jnp.concatenate(merged, axis=1), o_mem, x_ref, w_ref, g_ref,
                 out_ref)


def _outproj_merge(outs, lses, proj, mkv, x, w_o, g, *, tm):
    s, d = x.shape
    row = lambda w: pl.BlockSpec((tm, w), lambda i: (i, 0))
    full = lambda a: pl.BlockSpec(a.shape, lambda i: (0, 0))
    heads = lambda n, blk: pl.BlockSpec((n, tm, HEAD_DIM),
                                        lambda i: (blk, i, 0))
    qm_block = proj.shape[0] // MEM_HEADS - 1
    g2 = g.reshape(1, d)
    return pl.pallas_call(
        _outproj_merge_kernel,
        grid=(s // tm,),
        in_specs=[heads(HEADS_PER_GROUP, 0)] * 6
        + [heads(MEM_HEADS, qm_block), full(mkv), row(d), full(w_o), full(g2)],
        out_specs=row(d),
        out_shape=jax.ShapeDtypeStruct((s, d), F32),
        compiler_params=_params("parallel"),
        name="outproj_merge",
    )(*outs, *lses, proj, mkv, x, w_o, g2)


def _outproj_kernel(oa_ref, qm_ref, mkv_ref, x_ref, w_ref, g_ref, out_ref):
    queries = [qm_ref[:, hh * HEAD_DIM:(hh + 1) * HEAD_DIM]
               for hh in range(MEM_HEADS)]
    _project_out(oa_ref[...], _mem_attn(queries, mkv_ref), x_ref, w_ref, g_ref,
                 out_ref)


def _outproj(o_attn, proj, qm_tile, mkv, x, w_o, g, *, tm):
    s, d = x.shape
    row = lambda w: pl.BlockSpec((tm, w), lambda i: (i, 0))
    full = lambda a: pl.BlockSpec(a.shape, lambda i: (0, 0))
    g2 = g.reshape(1, d)
    return pl.pallas_call(
        _outproj_kernel,
        grid=(s // tm,),
        in_specs=[row(o_attn.shape[1]),
                  pl.BlockSpec((tm, MEM_W), lambda i: (i, qm_tile)), full(mkv),
                  row(d), full(w_o), full(g2)],
        out_specs=row(d),
        out_shape=jax.ShapeDtypeStruct((s, d), F32),
        compiler_params=_params("parallel"),
        name="outproj",
    )(o_attn, proj, mkv, x, w_o, g2)


def _ffn_kernel(x_ref, xh_ref, gin_ref, wg_ref, wv_ref, wc_ref, bc_ref, wd_ref,
                gout_ref, o_ref, hn_ref, hh_ref, gbuf_ref, act_ref, *, tm, nf):
    i = pl.program_id(0)
    j = pl.program_id(1)
    halo = SUBLANE_BF16

    def up(slot):
        wg = wg_ref[...]
        gate = _dot(hn_ref[0], wg)
        val = _dot(hn_ref[0], wv_ref[...])
        gate_halo = jnp.where(i > 0, _dot(hh_ref[...], wg), 0.0)
        gbuf_ref[0:halo, :] = gate_halo
        gbuf_ref[halo:halo + tm, :] = gate
        g_m1 = gbuf_ref[pl.ds(halo - 1, tm), :]
        g_m2 = gbuf_ref[pl.ds(halo - 2, tm), :]
        acc = bc_ref[...] + gate * wc_ref[2:3, :]
        acc = acc + g_m2 * wc_ref[0:1, :]
        acc = acc + g_m1 * wc_ref[1:2, :]
        act = (acc * (1.0 / (1.0 + jnp.exp(-acc)))) * val
        act_ref[slot] = act.astype(BF16)

    def down(slot):
        o_ref[...] += _dot(act_ref[slot], wd_ref[...])

    @pl.when(j == 0)
    def _():
        _norm_rows(x_ref, gin_ref, hn_ref)
        hh_ref[...] = _rms(xh_ref[...], gin_ref[...]).astype(BF16)
        o_ref[...] = jnp.zeros_like(o_ref)
        up(0)

    @pl.when((j > 0) & (j < nf))
    def _():
        slot = lax.rem(j, 2)
        down(1 - slot)
        up(slot)

    @pl.when(j == nf)
    def _():
        down((nf - 1) % 2)
        o_ref[...] = x_ref[...] + _rms(o_ref[...], gout_ref[...])


def _ffn(x, g_in, w_up, w_conv, b_conv, w_down, g_out, layer, *, tm, tf):
    s, d = x.shape
    depth, d_ff, _ = w_down.shape
    assert s % tm == 0 and d_ff % tf == 0 and tm % SUBLANE_BF16 == 0
    nf = d_ff // tf
    halo = SUBLANE_BF16
    up_j = lambda j: jnp.minimum(j, nf - 1)
    down_j = lambda j: jnp.maximum(j - 1, 0)
    return pl.pallas_call(
        functools.partial(_ffn_kernel, tm=tm, nf=nf),
        grid=(s // tm, nf + 1),
        in_specs=[
            pl.BlockSpec((tm, d), lambda i, j: (i, 0)),
            pl.BlockSpec((halo, d),
                         lambda i, j: (jnp.maximum(i * (tm // halo) - 1, 0), 0)),
            pl.BlockSpec((1, d), lambda i, j: (0, 0)),
            pl.BlockSpec((None, d, tf), lambda i, j: (layer, 0, up_j(j))),
            pl.BlockSpec((None, d, tf), lambda i, j: (layer, 0, nf + up_j(j))),
            pl.BlockSpec((None, CONV_WIDTH, tf),
                         lambda i, j: (layer, 0, up_j(j))),
            pl.BlockSpec((None, 1, tf), lambda i, j: (layer, 0, up_j(j))),
            pl.BlockSpec((None, tf, d), lambda i, j: (layer, down_j(j), 0)),
            pl.BlockSpec((1, d), lambda i, j: (0, 0)),
        ],
        out_specs=pl.BlockSpec((tm, d), lambda i, j: (i, 0)),
        out_shape=jax.ShapeDtypeStruct((s, d), F32),
        scratch_shapes=[pltpu.VMEM((1, tm, d), BF16),
                        pltpu.VMEM((halo, d), BF16),
                        pltpu.VMEM((halo + tm, tf), F32),
                        pltpu.VMEM((2, tm, tf), BF16)],
        compiler_params=_params("parallel", "arbitrary"),
        name="conv_ffn",
    )(x, x, g_in.reshape(1, d), w_up, w_up, w_conv,
      b_conv.reshape(depth, 1, d_ff), w_down, g_out.reshape(1, d))


def _sb_kernel(q_ref, k_ref, v_ref, o_ref, acc_ref, carry_ref, tri_ref, *, tq,
               heads):
    i = pl.program_id(1)
    scale = HEAD_DIM ** -0.5
    half = SB_SEGMENT
    n_seg = tq // half
    hsl = [slice(h * HEAD_DIM, (h + 1) * HEAD_DIM) for h in range(heads)]

    @pl.when(i == 0)
    def _():
        rr = lax.broadcasted_iota(jnp.int32, (2 * half, 2 * half), 0)
        cc = lax.broadcasted_iota(jnp.int32, (2 * half, 2 * half), 1)
        tri_ref[...] = jnp.where((cc >= half) | ((rr & (half - 1)) > cc),
                                 1.0, 0.0).astype(BF16)

    tri = tri_ref[...]
    qi = lax.broadcasted_iota(jnp.int32, (half, half), 0)
    kj = lax.broadcasted_iota(jnp.int32, (half, half), 1)
    strictly_earlier = kj < qi

    def run(segments, nrows, nk, diagonal_last):
        chains = [(slice(r0, r0 + nrows), pl.ds(ks, nk), s)
                  for r0, ks in segments for s in hsl]
        n_half = nk // half
        last = (n_half - 1) * half

        def mask_last(t):
            if not diagonal_last:
                return t
            tail = jnp.where(strictly_earlier, t[:, last:], 0.0)
            return tail if n_half == 1 else jnp.concatenate(
                [t[:, :last], tail], axis=1)

        zs = [_dot_nt(q_ref[r, s], k_ref[k, s]) * scale for r, k, s in chains]
        softs = [jnp.log(1.0 + jnp.exp2(jnp.abs(z) * NEG_LOG2E)) for z in zs]
        log_sigs = [jnp.minimum(z, 0.0) - t for z, t in zip(zs, softs)]
        drops = [mask_last(jnp.maximum(z, 0.0) + t) for z, t in zip(zs, softs)]
        his = [d.astype(BF16) for d in drops]
        los = [(d - hi.astype(F32)).astype(BF16) for d, hi in zip(drops, his)]
        first = diagonal_last
        carries = [None if first else carry_ref[r, s] for r, _, s in chains]
        parts = [[None] * n_half for _ in chains]
        for c in reversed(range(n_half)):
            sl = slice(c * half, (c + 1) * half)
            ress = [_dot(jnp.concatenate([hi[:, sl], lo[:, sl]], axis=1), tri)
                    for hi, lo in zip(his, los)]
            for n, res in enumerate(ress):
                if carries[n] is None:
                    parts[n][c] = jnp.exp(log_sigs[n][:, sl] - res[:, :half])
                    carries[n] = res[:, half:]
                else:
                    parts[n][c] = jnp.exp(log_sigs[n][:, sl]
                                          - (carries[n] + res[:, :half]))
                    carries[n] = carries[n] + res[:, half:]
        for n, (r, k, s) in enumerate(chains):
            a = mask_last(jnp.concatenate(parts[n], axis=1))
            pv = _dot(a.astype(BF16), v_ref[k, s])
            if first:
                acc_ref[r, s] = pv
            else:
                acc_ref[r, s] += pv
            carry_ref[r, s] = carries[n]

    window = SB_HISTORY + half
    pairs = [list(range(s, min(s + 2, n_seg))) for s in range(0, n_seg, 2)]

    @pl.when(i == 0)
    def _():
        short = [s for s in range(n_seg) if s * half < SB_HISTORY]
        for s in short:
            run([(s * half, 0)], half, (s + 1) * half, True)
        rest = [(s * half, s * half - SB_HISTORY)
                for s in range(n_seg) if s not in short]
        if rest:
            run(rest, half, window, True)

    @pl.when(i > 0)
    def _():
        k0 = pl.multiple_of(i * tq - SB_HISTORY, half)
        for pair in pairs:
            run([(s * half, k0 + s * half) for s in pair], half, window, True)

    def live():
        return jnp.min(carry_ref[...]) < -EXP_UNDERFLOW_F32

    alive = live()

    @pl.when(alive & (i > 0))
    def _():
        k0 = pl.multiple_of(i * tq - SB_HISTORY, half)
        for s in range(1, n_seg):
            run([(s * half, k0)], half, s * half, False)

    @pl.when(alive & (i == 0))
    def _():
        for s in range(n_seg):
            unseen = s * half - SB_HISTORY
            if unseen > 0:
                run([(s * half, 0)], half, unseen, False)

    def cond(c):
        step, alive = c
        return (step < n_rest) & alive

    def body(c):
        step, _ = c
        kstart = pl.multiple_of(i * tq - SB_HISTORY * (step + 2), SB_HISTORY)
        run([(0, kstart)], tq, SB_HISTORY, False)
        return step + 1, live()

    n_rest = jnp.maximum((i * tq - SB_HISTORY) // SB_HISTORY, 0)
    lax.while_loop(cond, body, (jnp.int32(0), alive))
    o_ref[...] = acc_ref[...].astype(o_ref.dtype)


def _stick_breaking(kvq, *, tq, heads):
    s = kvq.shape[0]
    assert SB_SEGMENT == HEAD_DIM and SB_HISTORY % SB_SEGMENT == 0
    assert tq % SB_HISTORY == 0 and s % tq == 0 and SB_HEADS % heads == 0
    groups = SB_HEADS // heads
    width = heads * HEAD_DIM
    whole = functools.partial(pl.BlockSpec, (s, width),
                              pipeline_mode=pl.Buffered(1))
    return pl.pallas_call(
        functools.partial(_sb_kernel, tq=tq, heads=heads),
        grid=(groups, s // tq),
        in_specs=[pl.BlockSpec((tq, width), lambda g, i: (i, 2 * groups + g)),
                  whole(lambda g, i: (0, g)),
                  whole(lambda g, i: (0, groups + g))],
        out_specs=pl.BlockSpec((tq, width), lambda g, i: (i, g)),
        out_shape=jax.ShapeDtypeStruct((s, SB_W), BF16),
        scratch_shapes=[pltpu.VMEM((tq, width), F32),
                        pltpu.VMEM((tq, width), F32),
                        pltpu.VMEM((2 * SB_SEGMENT, 2 * SB_SEGMENT), BF16)],
        compiler_params=_params("arbitrary", "arbitrary"),
        name="stick_breaking",
    )(kvq, kvq, kvq)


def _rope_tables(s):
    pos = jnp.arange(s, dtype=F32)
    inv = ROPE_THETA ** (-jnp.arange(0, HEAD_DIM, 2, dtype=F32) / HEAD_DIM)
    ang = pos[:, None] * inv[None, :]
    cos, sin = jnp.cos(ang), jnp.sin(ang)
    return (jnp.concatenate([cos, cos], axis=-1),
            jnp.concatenate([-sin, sin], axis=-1))


def kernel(x, mem, norms, w_in_a, w_o_a, g_kv, w_kv, w_in_b, w_o_b, w_mem_kv,
           w_up, w_conv, b_conv, w_down):
    b, s, d = x.shape
    assert b == 1 and norms.shape[0] == 2
    xs = x.reshape(s, d)
    mems = mem.reshape(mem.shape[1], d)
    bf = lambda a: a.astype(BF16)
    tm = min(1024, s)

    nrm = norms[0]
    cos2, sin2 = _rope_tables(s)
    proj = _norm_matmul(xs, nrm[0:1], bf(w_in_a[0]), tm=tm, tn=2 * GROUP_W,
                        out_dtype=F32, rope_cols=2 * DIL_W, cos2=cos2,
                        sin2=sin2)
    mkv = _norm_matmul(mems, nrm[4:5], bf(w_mem_kv[0]), tm=mems.shape[0],
                       tn=GROUP_W, out_dtype=BF16)
    outs, lses = [], []
    for gi, (_, dilation) in enumerate(DIL_GROUPS):
        o_g, l_g = _dilated_group(proj, gi, dilation)
        outs.append(o_g)
        lses.append(l_g)
    xs = _outproj_merge(outs, lses, proj, mkv, xs, bf(w_o_a[0]), nrm[1],
                        tm=min(512, s))
    w_up_bf, w_down_bf = bf(w_up), bf(w_down)
    xs = _ffn(xs, nrm[2], w_up_bf, w_conv, b_conv, w_down_bf, nrm[3], 0,
              tm=min(1024, s), tf=512)

    nrm = norms[1]
    kvq = _norm_matmul(xs, jnp.stack([g_kv, nrm[0]]),
                       bf(jnp.concatenate([w_kv, w_in_b[0]], axis=1)),
                       tm=tm, tn=2 * GROUP_W, out_dtype=BF16,
                       gain_split_col=2 * SB_W)
    mkv = _norm_matmul(mems, nrm[4:5], bf(w_mem_kv[1]), tm=mems.shape[0],
                       tn=GROUP_W, out_dtype=BF16)
    o_sb = _stick_breaking(kvq, tq=2 * SB_SEGMENT, heads=4)
    xs = _outproj(o_sb, kvq, 3 * SB_W // MEM_W, mkv, xs, bf(w_o_b[0]), nrm[1],
                  tm=min(512, s))
    xs = _ffn(xs, nrm[2], w_up_bf, w_conv, b_conv, w_down_bf, nrm[3], 1,
              tm=min(1024, s), tf=512)
    return xs.reshape(b, s, d)
```

```python
import functools

import jax
import jax.numpy as jnp
from jax import lax
from jax.experimental import pallas as pl
from jax.experimental.pallas import tpu as pltpu

HEAD_DIM = 128
MEM_HEADS = 4
DIL_GROUPS = ((128, 1), (512, 4), (2048, 16))
HEADS_PER_GROUP = 4
DIL_HEADS = len(DIL_GROUPS) * HEADS_PER_GROUP
SB_HEADS = 12
BLOCK = 128
CONV_WIDTH = 3
ROPE_THETA = 10000.0
EPS = 1e-6
NEG_INF = -1e30
EXP_UNDERFLOW_F32 = -105.0
NEG_LOG2E = -1.4426950408889634

DIL_W = DIL_HEADS * HEAD_DIM
MEM_W = MEM_HEADS * HEAD_DIM
SB_W = SB_HEADS * HEAD_DIM
GROUP_W = HEADS_PER_GROUP * HEAD_DIM
DIL_ROWS = 2 * BLOCK * max(d for _, d in DIL_GROUPS)
DIL_STAGE_CHAINS = {1: 16, 4: 16, 16: 8}
SB_SEGMENT = 128
SB_HISTORY = 256

F32 = jnp.float32
BF16 = jnp.bfloat16

VMEM_LIMIT_BYTES = 59 * 1024 * 1024
SUBLANE_BF16 = 16

PROJ_ROWS, PROJ_COLS = 1024, 1024
OUT_ROWS = 512
FFN_ROWS, FFN_COLS = 1024, 512
SB_HEADS_PER_STEP = 4


def _params(*sem):
    return pltpu.CompilerParams(dimension_semantics=sem,
                                vmem_limit_bytes=VMEM_LIMIT_BYTES)


def _rms(xf, g):
    y = xf * lax.rsqrt(jnp.mean(xf * xf, axis=-1, keepdims=True) + EPS)
    return y * g


def _dot(a, b):
    return jnp.dot(a, b, preferred_element_type=F32)


def _dot_nt(a, b):
    return lax.dot_general(a, b, (((1,), (1,)), ((), ())),
                           preferred_element_type=F32)


def _norm_rows(x_ref, g_ref, h_ref):
    xf = x_ref[...]
    y = xf * lax.rsqrt(jnp.mean(xf * xf, axis=-1, keepdims=True) + EPS)
    for gi in range(h_ref.shape[0]):
        h_ref[gi] = (y * g_ref[gi:gi + 1, :]).astype(BF16)


def _norm_matmul_kernel(*refs, rope_tiles, heads_per_tile, gain_split):
    if rope_tiles:
        x_ref, g_ref, w_ref, cos_ref, sin_ref, o_ref, h_ref = refs
    else:
        x_ref, g_ref, w_ref, o_ref, h_ref = refs
    j = pl.program_id(1)
    n_gains = h_ref.shape[0]

    @pl.when(j == 0)
    def _():
        _norm_rows(x_ref, g_ref, h_ref)

    if n_gains == 1:
        lhs = h_ref[0]
    else:
        lhs = h_ref[jnp.where(j >= gain_split, 1, 0)]
    acc = _dot(lhs, w_ref[...])

    if rope_tiles:
        is_rope = j < rope_tiles
        c = cos_ref[...]
        s = sin_ref[...]
        for hh in range(heads_per_tile):
            seg = acc[:, hh * HEAD_DIM:(hh + 1) * HEAD_DIM]
            rot = pltpu.roll(seg, HEAD_DIM // 2, 1)
            o_ref[hh] = jnp.where(is_rope, seg * c + rot * s,
                                  seg).astype(o_ref.dtype)
    else:
        o_ref[...] = acc.astype(o_ref.dtype)


def _norm_matmul(x, gains, w, *, tm, tn, out_dtype, gain_split_col=0,
                 rope_cols=0, cos2=None, sin2=None):
    m, k = x.shape
    n = w.shape[1]
    n_gains = gains.shape[0]
    assert m % tm == 0 and n % tn == 0
    assert rope_cols % tn == 0 and gain_split_col % tn == 0
    in_specs = [
        pl.BlockSpec((tm, k), lambda i, j: (i, 0)),
        pl.BlockSpec((n_gains, k), lambda i, j: (0, 0)),
        pl.BlockSpec((k, tn), lambda i, j: (0, j)),
    ]
    args = [x, gains, w]
    if rope_cols:
        in_specs += [pl.BlockSpec((tm, HEAD_DIM), lambda i, j: (i, 0))] * 2
        args += [cos2, sin2]
        out_spec = pl.BlockSpec((tn // HEAD_DIM, tm, HEAD_DIM),
                                lambda i, j: (j, i, 0))
        out_shape = jax.ShapeDtypeStruct((n // HEAD_DIM, m, HEAD_DIM), out_dtype)
    else:
        out_spec = pl.BlockSpec((tm, tn), lambda i, j: (i, j))
        out_shape = jax.ShapeDtypeStruct((m, n), out_dtype)
    return pl.pallas_call(
        functools.partial(_norm_matmul_kernel, rope_tiles=rope_cols // tn,
                          heads_per_tile=tn // HEAD_DIM,
                          gain_split=gain_split_col // tn),
        grid=(m // tm, n // tn),
        in_specs=in_specs,
        out_specs=out_spec,
        out_shape=out_shape,
        scratch_shapes=[pltpu.VMEM((n_gains, tm, k), BF16)],
        compiler_params=_params("parallel", "arbitrary"),
        name="norm_matmul",
    )(*args)


def _dilated_kernel(q_ref, kc_ref, vc_ref, kp_ref, vp_ref, o_ref, l_ref, *,
                    dilation, chains_per_stage):
    n = pl.program_id(0)
    scale = HEAD_DIM ** -0.5
    nq = DIL_ROWS // (BLOCK * dilation)
    qi = lax.broadcasted_iota(jnp.int32, (BLOCK, BLOCK), 0)
    kj = lax.broadcasted_iota(jnp.int32, (BLOCK, BLOCK), 1)
    band = kj >= qi
    valid_cur = kj <= qi

    def rows(ref, b, r):
        start = b * BLOCK * dilation + r
        if dilation == 1:
            return pl.ds(start, BLOCK)
        return pl.ds(start, BLOCK, stride=dilation)

    loaded = {}

    def kv(b, r):
        if (b, r) not in loaded:
            if b < 0:
                idx = rows(kp_ref, 0, r)
                loaded[(b, r)] = (kp_ref[idx, :].astype(BF16),
                                  vp_ref[idx, :].astype(BF16))
            else:
                idx = rows(kc_ref, b, r)
                loaded[(b, r)] = (kc_ref[idx, :].astype(BF16),
                                  vc_ref[idx, :].astype(BF16))
        return loaded[(b, r)]

    def run_stage(chains):
        qs, kps, kcs, vps, vcs, masks, idxs = [], [], [], [], [], [], []
        for b, r in chains:
            idx = rows(q_ref, b, r)
            qs.append(q_ref[idx, :].astype(BF16))
            kp, vp = kv(b - 1, r)
            kc, vc = kv(b, r)
            kps.append(kp); vps.append(vp); kcs.append(kc); vcs.append(vc)
            masks.append(band if b > 0 else band & (n > 0))
            idxs.append(idx)
        sp = [jnp.where(m, _dot_nt(q, k) * scale, NEG_INF)
              for q, k, m in zip(qs, kps, masks)]
        sc = [jnp.where(valid_cur, _dot_nt(q, k) * scale, NEG_INF)
              for q, k in zip(qs, kcs)]
        mx = [jnp.maximum(jnp.max(a, axis=-1, keepdims=True),
                          jnp.max(c, axis=-1, keepdims=True))
              for a, c in zip(sp, sc)]
        tot = [jnp.sum(jnp.exp(a - m), axis=-1, keepdims=True)
               + jnp.sum(jnp.exp(c - m), axis=-1, keepdims=True)
               for a, c, m in zip(sp, sc, mx)]
        lse = [m + jnp.log(t) for m, t in zip(mx, tot)]
        pp = [jnp.exp(a - l).astype(BF16) for a, l in zip(sp, lse)]
        pc = [jnp.exp(c - l).astype(BF16) for c, l in zip(sc, lse)]
        for idx, a, c, vp, vc, l in zip(idxs, pp, pc, vps, vcs, lse):
            o_ref[idx, :] = _dot(a, vp) + _dot(c, vc)
            l_ref[idx, :] = jnp.broadcast_to(l, (BLOCK, HEAD_DIM))

    chains = [(b, r) for b in range(nq) for r in range(dilation)]
    for c0 in range(0, len(chains), chains_per_stage):
        run_stage(chains[c0:c0 + chains_per_stage])


def _dilated_group(proj, gi, dilation):
    s = proj.shape[1]
    assert s % DIL_ROWS == 0 and DIL_ROWS % (BLOCK * dilation) == 0
    prev_rows = BLOCK * dilation
    ratio = DIL_ROWS // prev_rows
    q_col = gi * HEADS_PER_GROUP
    k_col = DIL_HEADS + q_col
    v_col = 2 * DIL_HEADS + q_col

    def cur(col):
        return pl.BlockSpec((None, DIL_ROWS, HEAD_DIM),
                            lambda n, h: (col + h, n, 0))

    def prev(col):
        return pl.BlockSpec(
            (None, prev_rows, HEAD_DIM),
            lambda n, h: (col + h, jnp.maximum(n * ratio - 1, 0), 0))

    out_spec = pl.BlockSpec((None, DIL_ROWS, HEAD_DIM), lambda n, h: (h, n, 0))
    return pl.pallas_call(
        functools.partial(_dilated_kernel, dilation=dilation,
                          chains_per_stage=DIL_STAGE_CHAINS[dilation]),
        grid=(s // DIL_ROWS, HEADS_PER_GROUP),
        in_specs=[cur(q_col), cur(k_col), cur(v_col), prev(k_col),
                  prev(v_col)],
        out_specs=[out_spec, out_spec],
        out_shape=[jax.ShapeDtypeStruct((HEADS_PER_GROUP, s, HEAD_DIM), F32)] * 2,
        compiler_params=_params("parallel", "arbitrary"),
        name=f"dilated_d{dilation}",
    )(proj, proj, proj, proj, proj)


def _mem_attn(queries, mkv_ref):
    scale = HEAD_DIM ** -0.5
    ss = [_dot_nt(q.astype(BF16),
                  mkv_ref[:, hh * HEAD_DIM:(hh + 1) * HEAD_DIM]) * scale
          for hh, q in enumerate(queries)]
    es = [jnp.exp(s - jnp.max(s, axis=-1, keepdims=True)) for s in ss]
    ps = [(e / jnp.sum(e, axis=-1, keepdims=True)).astype(BF16) for e in es]
    return jnp.concatenate(
        [_dot(p, mkv_ref[:, MEM_W + hh * HEAD_DIM:
                         MEM_W + (hh + 1) * HEAD_DIM]).astype(BF16)
         for hh, p in enumerate(ps)], axis=1)


def _project_out(attn, o_mem, x_ref, w_ref, g_ref, out_ref):
    split = attn.shape[1]
    y = _dot(attn, w_ref[0:split, :]) + _dot(o_mem, w_ref[split:, :])
    out_ref[...] = x_ref[...] + _rms(y, g_ref[...])


def _outproj_merge_kernel(o1, o2, o3, l1, l2, l3, qm_ref, mkv_ref, x_ref, w_ref,
                          g_ref, out_ref):
    merged = []
    for hh in range(HEADS_PER_GROUP):
        a1, a2, a3 = l1[hh], l2[hh], l3[hh]
        mx = jnp.maximum(jnp.maximum(a1, a2), a3)
        e1, e2, e3 = jnp.exp(a1 - mx), jnp.exp(a2 - mx), jnp.exp(a3 - mx)
        den = e1 + e2 + e3
        o_dil = (e1 / den) * o1[hh] + (e2 / den) * o2[hh] + (e3 / den) * o3[hh]
        merged.append(o_dil.astype(BF16))
    o_mem = _mem_attn([qm_ref[hh] for hh in range(MEM_HEADS)], mkv_ref)
    _project_out(jnp.concatenate(merged, axis=1), o_mem, x_ref, w_ref, g_ref,
                 out_ref)


def _outproj_merge(outs, lses, proj, mkv, x, w_o, g, *, tm):
    s, d = x.shape
    row = lambda w: pl.BlockSpec((tm, w), lambda i: (i, 0))
    full = lambda a: pl.BlockSpec(a.shape, lambda i: (0, 0))
    heads = lambda n, blk: pl.BlockSpec((n, tm, HEAD_DIM),
                                        lambda i: (blk, i, 0))
    qm_block = proj.shape[0] // MEM_HEADS - 1
    g2 = g.reshape(1, d)
    return pl.pallas_call(
        _outproj_merge_kernel,
        grid=(s // tm,),
        in_specs=[heads(HEADS_PER_GROUP, 0)] * 6
        + [heads(MEM_HEADS, qm_block), full(mkv), row(d), full(w_o), full(g2)],
        out_specs=row(d),
        out_shape=jax.ShapeDtypeStruct((s, d), F32),
        compiler_params=_params("parallel"),
        name="outproj_merge",
    )(*outs, *lses, proj, mkv, x, w_o, g2)


def _outproj_kernel(oa_ref, qm_ref, mkv_ref, x_ref, w_ref, g_ref, out_ref):
    queries = [qm_ref[:, hh * HEAD_DIM:(hh + 1) * HEAD_DIM]
               for hh in range(MEM_HEADS)]
    _project_out(oa_ref[...], _mem_attn(queries, mkv_ref), x_ref, w_ref, g_ref,
                 out_ref)


def _outproj(o_attn, proj, qm_tile, mkv, x, w_o, g, *, tm):
    s, d = x.shape
    row = lambda w: pl.BlockSpec((tm, w), lambda i: (i, 0))
    full = lambda a: pl.BlockSpec(a.shape, lambda i: (0, 0))
    g2 = g.reshape(1, d)
    return pl.pallas_call(
        _outproj_kernel,
        grid=(s // tm,),
        in_specs=[row(o_attn.shape[1]),
                  pl.BlockSpec((tm, MEM_W), lambda i: (i, qm_tile)), full(mkv),
                  row(d), full(w_o), full(g2)],
        out_specs=row(d),
        out_shape=jax.ShapeDtypeStruct((s, d), F32),
        compiler_params=_params("parallel"),
        name="outproj",
    )(o_attn, proj, mkv, x, w_o, g2)


def _ffn_kernel(x_ref, xh_ref, gin_ref, wg_ref, wv_ref, wc_ref, bc_ref, wd_ref,
                gout_ref, o_ref, hn_ref, hh_ref, gbuf_ref, act_ref, *, tm, nf):
    i = pl.program_id(0)
    j = pl.program_id(1)
    halo = SUBLANE_BF16

    def up(slot):
        wg = wg_ref[...]
        gate = _dot(hn_ref[0], wg)
        val = _dot(hn_ref[0], wv_ref[...])
        gate_halo = jnp.where(i > 0, _dot(hh_ref[...], wg), 0.0)
        gbuf_ref[0:halo, :] = gate_halo
        gbuf_ref[halo:halo + tm, :] = gate
        g_m1 = gbuf_ref[pl.ds(halo - 1, tm), :]
        g_m2 = gbuf_ref[pl.ds(halo - 2, tm), :]
        acc = bc_ref[...] + gate * wc_ref[2:3, :]
        acc = acc + g_m2 * wc_ref[0:1, :]
        acc = acc + g_m1 * wc_ref[1:2, :]
        act = (acc * (1.0 / (1.0 + jnp.exp(-acc)))) * val
        act_ref[slot] = act.astype(BF16)

    def down(slot):
        o_ref[...] += _dot(act_ref[slot], wd_ref[...])

    @pl.when(j == 0)
    def _():
        _norm_rows(x_ref, gin_ref, hn_ref)
        hh_ref[...] = _rms(xh_ref[...], gin_ref[...]).astype(BF16)
        o_ref[...] = jnp.zeros_like(o_ref)
        up(0)

    @pl.when((j > 0) & (j < nf))
    def _():
        slot = lax.rem(j, 2)
        down(1 - slot)
        up(slot)

    @pl.when(j == nf)
    def _():
        down((nf - 1) % 2)
        o_ref[...] = x_ref[...] + _rms(o_ref[...], gout_ref[...])


def _ffn(x, g_in, w_up, w_conv, b_conv, w_down, g_out, layer, *, tm, tf):
    s, d = x.shape
    depth, d_ff, _ = w_down.shape
    assert s % tm == 0 and d_ff % tf == 0 and tm % SUBLANE_BF16 == 0
    nf = d_ff // tf
    halo = SUBLANE_BF16
    up_j = lambda j: jnp.minimum(j, nf - 1)
    down_j = lambda j: jnp.maximum(j - 1, 0)
    return pl.pallas_call(
        functools.partial(_ffn_kernel, tm=tm, nf=nf),
        grid=(s // tm, nf + 1),
        in_specs=[
            pl.BlockSpec((tm, d), lambda i, j: (i, 0)),
            pl.BlockSpec((halo, d),
                         lambda i, j: (jnp.maximum(i * (tm // halo) - 1, 0), 0)),
            pl.BlockSpec((1, d), lambda i, j: (0, 0)),
            pl.BlockSpec((None, d, tf), lambda i, j: (layer, 0, up_j(j))),
            pl.BlockSpec((None, d, tf), lambda i, j: (layer, 0, nf + up_j(j))),
            pl.BlockSpec((None, CONV_WIDTH, tf),
                         lambda i, j: (layer, 0, up_j(j))),
            pl.BlockSpec((None, 1, tf), lambda i, j: (layer, 0, up_j(j))),
            pl.BlockSpec((None, tf, d), lambda i, j: (layer, down_j(j), 0)),
            pl.BlockSpec((1, d), lambda i, j: (0, 0)),
        ],
        out_specs=pl.BlockSpec((tm, d), lambda i, j: (i, 0)),
        out_shape=jax.ShapeDtypeStruct((s, d), F32),
        scratch_shapes=[pltpu.VMEM((1, tm, d), BF16),
                        pltpu.VMEM((halo, d), BF16),
                        pltpu.VMEM((halo + tm, tf), F32),
                        pltpu.VMEM((2, tm, tf), BF16)],
        compiler_params=_params("parallel", "arbitrary"),
        name="conv_ffn",
    )(x, x, g_in.reshape(1, d), w_up, w_up, w_conv,
      b_conv.reshape(depth, 1, d_ff), w_down, g_out.reshape(1, d))


def _sb_kernel(q_ref, k_ref, v_ref, o_ref, acc_ref, carry_ref, tri_ref, *, tq,
               heads):
    i = pl.program_id(1)
    scale = HEAD_DIM ** -0.5
    half = SB_SEGMENT
    n_seg = tq // half
    hsl = [slice(h * HEAD_DIM, (h + 1) * HEAD_DIM) for h in range(heads)]

    @pl.when(i == 0)
    def _():
        rr = lax.broadcasted_iota(jnp.int32, (2 * half, 2 * half), 0)
        cc = lax.broadcasted_iota(jnp.int32, (2 * half, 2 * half), 1)
        tri_ref[...] = jnp.where((cc >= half) | ((rr & (half - 1)) > cc),
                                 1.0, 0.0).astype(BF16)

    tri = tri_ref[...]
    qi = lax.broadcasted_iota(jnp.int32, (half, half), 0)
    kj = lax.broadcasted_iota(jnp.int32, (half, half), 1)
    strictly_earlier = kj < qi

    def run(segments, nrows, nk, diagonal_last):
        chains = [(slice(r0, r0 + nrows), pl.ds(ks, nk), s)
                  for r0, ks in segments for s in hsl]
        n_half = nk // half
        last = (n_half - 1) * half

        def mask_last(t):
            if not diagonal_last:
                return t
            tail = jnp.where(strictly_earlier, t[:, last:], 0.0)
            return tail if n_half == 1 else jnp.concatenate(
                [t[:, :last], tail], axis=1)

        zs = [_dot_nt(q_ref[r, s], k_ref[k, s]) * scale for r, k, s in chains]
        softs = [jnp.log(1.0 + jnp.exp2(jnp.abs(z) * NEG_LOG2E)) for z in zs]
        log_sigs = [jnp.minimum(z, 0.0) - t for z, t in zip(zs, softs)]
        drops = [mask_last(jnp.maximum(z, 0.0) + t) for z, t in zip(zs, softs)]
        his = [d.astype(BF16) for d in drops]
        los = [(d - hi.astype(F32)).astype(BF16) for d, hi in zip(drops, his)]
        first = diagonal_last
        carries = [None if first else carry_ref[r, s] for r, _, s in chains]
        parts = [[None] * n_half for _ in chains]
        for c in reversed(range(n_half)):
            sl = slice(c * half, (c + 1) * half)
            ress = [_dot(jnp.concatenate([hi[:, sl], lo[:, sl]], axis=1), tri)
                    for hi, lo in zip(his, los)]
            for n, res in enumerate(ress):
                if carries[n] is None:
                    parts[n][c] = jnp.exp(log_sigs[n][:, sl] - res[:, :half])
                    carries[n] = res[:, half:]
                else:
                    parts[n][c] = jnp.exp(log_sigs[n][:, sl]
                                          - (carries[n] + res[:, :half]))
                    carries[n] = carries[n] + res[:, half:]
        for n, (r, k, s) in enumerate(chains):
            a = mask_last(jnp.concatenate(parts[n], axis=1))
            pv = _dot(a.astype(BF16), v_ref[k, s])
            if first:
                acc_ref[r, s] = pv
            else:
                acc_ref[r, s] += pv
            carry_ref[r, s] = carries[n]

    window = SB_HISTORY + half
    pairs = [list(range(s, min(s + 2, n_seg))) for s in range(0, n_seg, 2)]

    @pl.when(i == 0)
    def _():
        short = [s for s in range(n_seg) if s * half < SB_HISTORY]
        for s in short:
            run([(s * half, 0)], half, (s + 1) * half, True)
        rest = [(s * half, s * half - SB_HISTORY)
                for s in range(n_seg) if s not in short]
        if rest:
            run(rest, half, window, True)

    @pl.when(i > 0)
    def _():
        k0 = pl.multiple_of(i * tq - SB_HISTORY, half)
        for pair in pairs:
            run([(s * half, k0 + s * half) for s in pair], half, window, True)

    def live():
        return jnp.min(carry_ref[...]) < -EXP_UNDERFLOW_F32

    alive = live()

    @pl.when(alive & (i > 0))
    def _():
        k0 = pl.multiple_of(i * tq - SB_HISTORY, half)
        for s in range(1, n_seg):
            run([(s * half, k0)], half, s * half, False)

    @pl.when(alive & (i == 0))
    def _():
        for s in range(n_seg):
            unseen = s * half - SB_HISTORY
            if unseen > 0:
                run([(s * half, 0)], half, unseen, False)

    def cond(c):
        step, alive = c
        return (step < n_rest) & alive

    def body(c):
        step, _ = c
        kstart = pl.multiple_of(i * tq - SB_HISTORY * (step + 2), SB_HISTORY)
        run([(0, kstart)], tq, SB_HISTORY, False)
        return step + 1, live()

    n_rest = jnp.maximum((i * tq - SB_HISTORY) // SB_HISTORY, 0)
    lax.while_loop(cond, body, (jnp.int32(0), alive))
    o_ref[...] = acc_ref[...].astype(o_ref.dtype)


def _stick_breaking(kvq, *, tq, heads):
    s = kvq.shape[0]
    assert SB_SEGMENT == HEAD_DIM and SB_HISTORY % SB_SEGMENT == 0
    assert tq % SB_HISTORY == 0 and s % tq == 0 and SB_HEADS % heads == 0
    groups = SB_HEADS // heads
    width = heads * HEAD_DIM
    whole = functools.partial(pl.BlockSpec, (s, width),
                              pipeline_mode=pl.Buffered(1))
    return pl.pallas_call(
        functools.partial(_sb_kernel, tq=tq, heads=heads),
        grid=(groups, s // tq),
        in_specs=[pl.BlockSpec((tq, width), lambda g, i: (i, 2 * groups + g)),
                  whole(lambda g, i: (0, g)),
                  whole(lambda g, i: (0, groups + g))],
        out_specs=pl.BlockSpec((tq, width), lambda g, i: (i, g)),
        out_shape=jax.ShapeDtypeStruct((s, SB_W), BF16),
        scratch_shapes=[pltpu.VMEM((tq, width), F32),
                        pltpu.VMEM((tq, width), F32),
                        pltpu.VMEM((2 * SB_SEGMENT, 2 * SB_SEGMENT), BF16)],
        compiler_params=_params("arbitrary", "arbitrary"),
        name="stick_breaking",
    )(kvq, kvq, kvq)


def _rope_tables(s):
    pos = jnp.arange(s, dtype=F32)
    inv = ROPE_THETA ** (-jnp.arange(0, HEAD_DIM, 2, dtype=F32) / HEAD_DIM)
    ang = pos[:, None] * inv[None, :]
    cos, sin = jnp.cos(ang), jnp.sin(ang)
    return (jnp.concatenate([cos, cos], axis=-1),
            jnp.concatenate([-sin, sin], axis=-1))


def kernel(x, mem, norms, w_in_a, w_o_a, g_kv, w_kv, w_in_b, w_o_b, w_mem_kv,
           w_up, w_conv, b_conv, w_down):
    b, s, d = x.shape
    assert b == 1 and norms.shape[0] == 2
    xs = x.reshape(s, d)
    mems = mem.reshape(mem.shape[1], d)
    bf = lambda a: a.astype(BF16)
    proj_rows = min(PROJ_ROWS, s)
    out_rows = min(OUT_ROWS, s)
    ffn_rows = min(FFN_ROWS, s)

    nrm = norms[0]
    cos2, sin2 = _rope_tables(s)
    proj = _norm_matmul(xs, nrm[0:1], bf(w_in_a[0]), tm=proj_rows,
                        tn=PROJ_COLS, out_dtype=F32, rope_cols=2 * DIL_W,
                        cos2=cos2, sin2=sin2)
    mkv = _norm_matmul(mems, nrm[4:5], bf(w_mem_kv[0]), tm=mems.shape[0],
                       tn=GROUP_W, out_dtype=BF16)
    outs, lses = [], []
    for gi, (_, dilation) in enumerate(DIL_GROUPS):
        o_g, l_g = _dilated_group(proj, gi, dilation)
        outs.append(o_g)
        lses.append(l_g)
    xs = _outproj_merge(outs, lses, proj, mkv, xs, bf(w_o_a[0]), nrm[1],
                        tm=out_rows)
    w_up_bf, w_down_bf = bf(w_up), bf(w_down)
    xs = _ffn(xs, nrm[2], w_up_bf, w_conv, b_conv, w_down_bf, nrm[3], 0,
              tm=ffn_rows, tf=FFN_COLS)

    nrm = norms[1]
    kvq = _norm_matmul(xs, jnp.stack([g_kv, nrm[0]]),
                       bf(jnp.concatenate([w_kv, w_in_b[0]], axis=1)),
                       tm=proj_rows, tn=PROJ_COLS, out_dtype=BF16,
                       gain_split_col=2 * SB_W)
    mkv = _norm_matmul(mems, nrm[4:5], bf(w_mem_kv[1]), tm=mems.shape[0],
                       tn=GROUP_W, out_dtype=BF16)
    o_sb = _stick_breaking(kvq, tq=2 * SB_SEGMENT, heads=SB_HEADS_PER_STEP)
    xs = _outproj(o_sb, kvq, 3 * SB_W // MEM_W, mkv, xs, bf(w_o_b[0]), nrm[1],
                  tm=out_rows)
    xs = _ffn(xs, nrm[2], w_up_bf, w_conv, b_conv, w_down_bf, nrm[3], 1,
              tm=ffn_rows, tf=FFN_COLS)
    return xs.reshape(b, s, d)
```

```python
import functools

import jax
import jax.numpy as jnp
from jax import lax
from jax.experimental import pallas as pl
from jax.experimental.pallas import tpu as pltpu

HEAD_DIM = 128
MEM_HEADS = 4
DIL_GROUPS = ((128, 1), (512, 4), (2048, 16))
HEADS_PER_GROUP = 4
DIL_HEADS = len(DIL_GROUPS) * HEADS_PER_GROUP
SB_HEADS = 12
BLOCK = 128
CONV_WIDTH = 3
ROPE_THETA = 10000.0
EPS = 1e-6
NEG_INF = -1e30
EXP_UNDERFLOW_F32 = -105.0
NEG_LOG2E = -1.4426950408889634

DIL_W = DIL_HEADS * HEAD_DIM
MEM_W = MEM_HEADS * HEAD_DIM
SB_W = SB_HEADS * HEAD_DIM
GROUP_W = HEADS_PER_GROUP * HEAD_DIM
DIL_ROWS = 2 * BLOCK * max(d for _, d in DIL_GROUPS)
DIL_STAGE_CHAINS = {1: 16, 4: 16, 16: 8}
SB_SEGMENT = 128
SB_HISTORY = 256

F32 = jnp.float32
BF16 = jnp.bfloat16

VMEM_LIMIT_BYTES = 59 * 1024 * 1024
SUBLANE_BF16 = 16

PROJ_ROWS, PROJ_COLS = 1024, 1024
OUT_ROWS = 512
FFN_ROWS, FFN_COLS = 1024, 512
SB_HEADS_PER_STEP = 4


def _params(*sem):
    return pltpu.CompilerParams(dimension_semantics=sem,
                                vmem_limit_bytes=VMEM_LIMIT_BYTES)


def _rms(xf, g):
    y = xf * lax.rsqrt(jnp.mean(xf * xf, axis=-1, keepdims=True) + EPS)
    return y * g


def _dot(a, b):
    return jnp.dot(a, b, preferred_element_type=F32)


def _dot_nt(a, b):
    return lax.dot_general(a, b, (((1,), (1,)), ((), ())),
                           preferred_element_type=F32)


def _norm_rows(x_ref, g_ref, h_ref):
    xf = x_ref[...]
    y = xf * lax.rsqrt(jnp.mean(xf * xf, axis=-1, keepdims=True) + EPS)
    for gi in range(h_ref.shape[0]):
        h_ref[gi] = (y * g_ref[gi:gi + 1, :]).astype(BF16)


def _norm_matmul_kernel(*refs, rope_tiles, heads_per_tile, gain_split):
    if rope_tiles:
        x_ref, g_ref, w_ref, cos_ref, sin_ref, o_ref, h_ref = refs
    else:
        x_ref, g_ref, w_ref, o_ref, h_ref = refs
    j = pl.program_id(1)
    n_gains = h_ref.shape[0]

    @pl.when(j == 0)
    def _():
        _norm_rows(x_ref, g_ref, h_ref)

    if n_gains == 1:
        lhs = h_ref[0]
    else:
        lhs = h_ref[jnp.where(j >= gain_split, 1, 0)]
    acc = _dot(lhs, w_ref[...])

    if rope_tiles:
        is_rope = j < rope_tiles
        c = cos_ref[...]
        s = sin_ref[...]
        for hh in range(heads_per_tile):
            seg = acc[:, hh * HEAD_DIM:(hh + 1) * HEAD_DIM]
            rot = pltpu.roll(seg, HEAD_DIM // 2, 1)
            o_ref[hh] = jnp.where(is_rope, seg * c + rot * s,
                                  seg).astype(o_ref.dtype)
    else:
        o_ref[...] = acc.astype(o_ref.dtype)


def _norm_matmul(x, gains, w, *, tm, tn, out_dtype, gain_split_col=0,
                 rope_cols=0, cos2=None, sin2=None):
    m, k = x.shape
    n = w.shape[1]
    n_gains = gains.shape[0]
    assert m % tm == 0 and n % tn == 0
    assert rope_cols % tn == 0 and gain_split_col % tn == 0
    in_specs = [
        pl.BlockSpec((tm, k), lambda i, j: (i, 0)),
        pl.BlockSpec((n_gains, k), lambda i, j: (0, 0)),
        pl.BlockSpec((k, tn), lambda i, j: (0, j)),
    ]
    args = [x, gains, w]
    if rope_cols:
        in_specs += [pl.BlockSpec((tm, HEAD_DIM), lambda i, j: (i, 0))] * 2
        args += [cos2, sin2]
        out_spec = pl.BlockSpec((tn // HEAD_DIM, tm, HEAD_DIM),
                                lambda i, j: (j, i, 0))
        out_shape = jax.ShapeDtypeStruct((n // HEAD_DIM, m, HEAD_DIM), out_dtype)
    else:
        out_spec = pl.BlockSpec((tm, tn), lambda i, j: (i, j))
        out_shape = jax.ShapeDtypeStruct((m, n), out_dtype)
    return pl.pallas_call(
        functools.partial(_norm_matmul_kernel, rope_tiles=rope_cols // tn,
                          heads_per_tile=tn // HEAD_DIM,
                          gain_split=gain_split_col // tn),
        grid=(m // tm, n // tn),
        in_specs=in_specs,
        out_specs=out_spec,
        out_shape=out_shape,
        scratch_shapes=[pltpu.VMEM((n_gains, tm, k), BF16)],
        compiler_params=_params("parallel", "arbitrary"),
        name="norm_matmul",
    )(*args)


def _dilated_kernel(q_ref, kc_ref, vc_ref, kp_ref, vp_ref, o_ref, l_ref, *,
                    dilation, chains_per_stage):
    n = pl.program_id(0)
    scale = HEAD_DIM ** -0.5
    nq = DIL_ROWS // (BLOCK * dilation)
    qi = lax.broadcasted_iota(jnp.int32, (BLOCK, BLOCK), 0)
    kj = lax.broadcasted_iota(jnp.int32, (BLOCK, BLOCK), 1)
    band = kj >= qi
    valid_cur = kj <= qi

    def rows(ref, b, r):
        start = b * BLOCK * dilation + r
        if dilation == 1:
            return pl.ds(start, BLOCK)
        return pl.ds(start, BLOCK, stride=dilation)

    loaded = {}

    def kv(b, r):
        if (b, r) not in loaded:
            if b < 0:
                idx = rows(kp_ref, 0, r)
                loaded[(b, r)] = (kp_ref[idx, :].astype(BF16),
                                  vp_ref[idx, :].astype(BF16))
            else:
                idx = rows(kc_ref, b, r)
                loaded[(b, r)] = (kc_ref[idx, :].astype(BF16),
                                  vc_ref[idx, :].astype(BF16))
        return loaded[(b, r)]

    def run_stage(chains):
        qs, kps, kcs, vps, vcs, masks, idxs = [], [], [], [], [], [], []
        for b, r in chains:
            idx = rows(q_ref, b, r)
            qs.append(q_ref[idx, :].astype(BF16))
            kp, vp = kv(b - 1, r)
            kc, vc = kv(b, r)
            kps.append(kp); vps.append(vp); kcs.append(kc); vcs.append(vc)
            masks.append(band if b > 0 else band & (n > 0))
            idxs.append(idx)
        sp = [jnp.where(m, _dot_nt(q, k) * scale, NEG_INF)
              for q, k, m in zip(qs, kps, masks)]
        sc = [jnp.where(valid_cur, _dot_nt(q, k) * scale, NEG_INF)
              for q, k in zip(qs, kcs)]
        mx = [jnp.maximum(jnp.max(a, axis=-1, keepdims=True),
                          jnp.max(c, axis=-1, keepdims=True))
              for a, c in zip(sp, sc)]
        tot = [jnp.sum(jnp.exp(a - m), axis=-1, keepdims=True)
               + jnp.sum(jnp.exp(c - m), axis=-1, keepdims=True)
               for a, c, m in zip(sp, sc, mx)]
        lse = [m + jnp.log(t) for m, t in zip(mx, tot)]
        pp = [jnp.exp(a - l).astype(BF16) for a, l in zip(sp, lse)]
        pc = [jnp.exp(c - l).astype(BF16) for c, l in zip(sc, lse)]
        for idx, a, c, vp, vc, l in zip(idxs, pp, pc, vps, vcs, lse):
            o_ref[idx, :] = _dot(a, vp) + _dot(c, vc)
            l_ref[idx, :] = jnp.broadcast_to(l, (BLOCK, HEAD_DIM))

    chains = [(b, r) for b in range(nq) for r in range(dilation)]
    for c0 in range(0, len(chains), chains_per_stage):
        run_stage(chains[c0:c0 + chains_per_stage])


def _dilated_group(proj, gi, dilation):
    s = proj.shape[1]
    assert s % DIL_ROWS == 0 and DIL_ROWS % (BLOCK * dilation) == 0
    prev_rows = BLOCK * dilation
    ratio = DIL_ROWS // prev_rows
    q_col = gi * HEADS_PER_GROUP
    k_col = DIL_HEADS + q_col
    v_col = 2 * DIL_HEADS + q_col

    def cur(col):
        return pl.BlockSpec((None, DIL_ROWS, HEAD_DIM),
                            lambda n, h: (col + h, n, 0))

    def prev(col):
        return pl.BlockSpec(
            (None, prev_rows, HEAD_DIM),
            lambda n, h: (col + h, jnp.maximum(n * ratio - 1, 0), 0))

    out_spec = pl.BlockSpec((None, DIL_ROWS, HEAD_DIM), lambda n, h: (h, n, 0))
    return pl.pallas_call(
        functools.partial(_dilated_kernel, dilation=dilation,
                          chains_per_stage=DIL_STAGE_CHAINS[dilation]),
        grid=(s // DIL_ROWS, HEADS_PER_GROUP),
        in_specs=[cur(q_col), cur(k_col), cur(v_col), prev(k_col),
                  prev(v_col)],
        out_specs=[out_spec, out_spec],
        out_shape=[jax.ShapeDtypeStruct((HEADS_PER_GROUP, s, HEAD_DIM), F32)] * 2,
        compiler_params=_params("parallel", "arbitrary"),
        name=f"dilated_d{dilation}",
    )(proj, proj, proj, proj, proj)


def _mem_attn(queries, mkv_ref):
    scale = HEAD_DIM ** -0.5
    ss = [_dot_nt(q.astype(BF16),
                  mkv_ref[:, hh * HEAD_DIM:(hh + 1) * HEAD_DIM]) * scale
          for hh, q in enumerate(queries)]
    es = [jnp.exp(s - jnp.max(s, axis=-1, keepdims=True)) for s in ss]
    ps = [(e / jnp.sum(e, axis=-1, keepdims=True)).astype(BF16) for e in es]
    return jnp.concatenate(
        [_dot(p, mkv_ref[:, MEM_W + hh * HEAD_DIM:
                         MEM_W + (hh + 1) * HEAD_DIM]).astype(BF16)
         for hh, p in enumerate(ps)], axis=1)


def _project_out(attn, o_mem, x_ref, w_ref, g_ref, out_ref):
    split = attn.shape[1]
    y = _dot(attn, w_ref[0:split, :]) + _dot(o_mem, w_ref[split:, :])
    out_ref[...] = x_ref[...] + _rms(y, g_ref[...])


def _outproj_merge_kernel(o1, o2, o3, l1, l2, l3, qm_ref, mkv_ref, x_ref, w_ref,
                          g_ref, out_ref):
    merged = []
    for hh in range(HEADS_PER_GROUP):
        a1, a2, a3 = l1[hh], l2[hh], l3[hh]
        mx = jnp.maximum(jnp.maximum(a1, a2), a3)
        e1, e2, e3 = jnp.exp(a1 - mx), jnp.exp(a2 - mx), jnp.exp(a3 - mx)
        den = e1 + e2 + e3
        o_dil = (e1 / den) * o1[hh] + (e2 / den) * o2[hh] + (e3 / den) * o3[hh]
        merged.append(o_dil.astype(BF16))
    o_mem = _mem_attn([qm_ref[hh] for hh in range(MEM_HEADS)], mkv_ref)
    _project_out(jnp.concatenate(merged, axis=1), o_mem, x_ref, w_ref, g_ref,
                 out_ref)


def _outproj_merge(outs, lses, proj, mkv, x, w_o, g, *, tm):
    s, d = x.shape
    row = lambda w: pl.BlockSpec((tm, w), lambda i: (i, 0))
    full = lambda a: pl.BlockSpec(a.shape, lambda i: (0, 0))
    heads = lambda n, blk: pl.BlockSpec((n, tm, HEAD_DIM),
                                        lambda i: (blk, i, 0))
    qm_block = proj.shape[0] // MEM_HEADS - 1
    g2 = g.reshape(1, d)
    return pl.pallas_call(
        _outproj_merge_kernel,
        grid=(s // tm,),
        in_specs=[heads(HEADS_PER_GROUP, 0)] * 6
        + [heads(MEM_HEADS, qm_block), full(mkv), row(d), full(w_o), full(g2)],
        out_specs=row(d),
        out_shape=jax.ShapeDtypeStruct((s, d), F32),
        compiler_params=_params("parallel"),
        name="outproj_merge",
    )(*outs, *lses, proj, mkv, x, w_o, g2)


def _outproj_kernel(oa_ref, qm_ref, mkv_ref, x_ref, w_ref, g_ref, out_ref):
    queries = [qm_ref[:, hh * HEAD_DIM:(hh + 1) * HEAD_DIM]
               for hh in range(MEM_HEADS)]
    _project_out(oa_ref[...], _mem_attn(queries, mkv_ref), x_ref, w_ref, g_ref,
                 out_ref)


def _outproj(o_attn, proj, qm_tile, mkv, x, w_o, g, *, tm):
    s, d = x.shape
    row = lambda w: pl.BlockSpec((tm, w), lambda i: (i, 0))
    full = lambda a: pl.BlockSpec(a.shape, lambda i: (0, 0))
    g2 = g.reshape(1, d)
    return pl.pallas_call(
        _outproj_kernel,
        grid=(s // tm,),
        in_specs=[row(o_attn.shape[1]),
                  pl.BlockSpec((tm, MEM_W), lambda i: (i, qm_tile)), full(mkv),
                  row(d), full(w_o), full(g2)],
        out_specs=row(d),
        out_shape=jax.ShapeDtypeStruct((s, d), F32),
        compiler_params=_params("parallel"),
        name="outproj",
    )(o_attn, proj, mkv, x, w_o, g2)


def _ffn_kernel(x_ref, xh_ref, gin_ref, wg_ref, wv_ref, wc_ref, bc_ref, wd_ref,
                gout_ref, o_ref, hn_ref, gbuf_ref, act_ref, *, tm, nf):
    i = pl.program_id(0)
    j = pl.program_id(1)
    halo = SUBLANE_BF16

    def up(slot):
        gate_all = _dot(hn_ref[...], wg_ref[...])
        val = _dot(hn_ref[halo:, :], wv_ref[...])
        gate = gate_all[halo:, :]
        gbuf_ref[0:halo, :] = jnp.where(i > 0, gate_all[0:halo, :], 0.0)
        gbuf_ref[halo:halo + tm, :] = gate
        g_m1 = gbuf_ref[pl.ds(halo - 1, tm), :]
        g_m2 = gbuf_ref[pl.ds(halo - 2, tm), :]
        acc = bc_ref[...] + gate * wc_ref[2:3, :]
        acc = acc + g_m2 * wc_ref[0:1, :]
        acc = acc + g_m1 * wc_ref[1:2, :]
        act = (acc * (1.0 / (1.0 + jnp.exp(-acc)))) * val
        act_ref[slot] = act.astype(BF16)

    def down(slot):
        o_ref[...] += _dot(act_ref[slot], wd_ref[...])

    @pl.when(j == 0)
    def _():
        hn_ref[0:halo, :] = _rms(xh_ref[...], gin_ref[...]).astype(BF16)
        hn_ref[halo:, :] = _rms(x_ref[...], gin_ref[...]).astype(BF16)
        o_ref[...] = jnp.zeros_like(o_ref)
        up(0)

    @pl.when((j > 0) & (j < nf))
    def _():
        slot = lax.rem(j, 2)
        down(1 - slot)
        up(slot)

    @pl.when(j == nf)
    def _():
        down((nf - 1) % 2)
        o_ref[...] = x_ref[...] + _rms(o_ref[...], gout_ref[...])


def _ffn(x, g_in, w_up, w_conv, b_conv, w_down, g_out, layer, *, tm, tf):
    s, d = x.shape
    depth, d_ff, _ = w_down.shape
    assert s % tm == 0 and d_ff % tf == 0 and tm % SUBLANE_BF16 == 0
    nf = d_ff // tf
    halo = SUBLANE_BF16
    up_j = lambda j: jnp.minimum(j, nf - 1)
    down_j = lambda j: jnp.maximum(j - 1, 0)
    return pl.pallas_call(
        functools.partial(_ffn_kernel, tm=tm, nf=nf),
        grid=(s // tm, nf + 1),
        in_specs=[
            pl.BlockSpec((tm, d), lambda i, j: (i, 0)),
            pl.BlockSpec((halo, d),
                         lambda i, j: (jnp.maximum(i * (tm // halo) - 1, 0), 0)),
            pl.BlockSpec((1, d), lambda i, j: (0, 0)),
            pl.BlockSpec((None, d, tf), lambda i, j: (layer, 0, up_j(j))),
            pl.BlockSpec((None, d, tf), lambda i, j: (layer, 0, nf + up_j(j))),
            pl.BlockSpec((None, CONV_WIDTH, tf),
                         lambda i, j: (layer, 0, up_j(j))),
            pl.BlockSpec((None, 1, tf), lambda i, j: (layer, 0, up_j(j))),
            pl.BlockSpec((None, tf, d), lambda i, j: (layer, down_j(j), 0)),
            pl.BlockSpec((1, d), lambda i, j: (0, 0)),
        ],
        out_specs=pl.BlockSpec((tm, d), lambda i, j: (i, 0)),
        out_shape=jax.ShapeDtypeStruct((s, d), F32),
        scratch_shapes=[pltpu.VMEM((halo + tm, d), BF16),
                        pltpu.VMEM((halo + tm, tf), F32),
                        pltpu.VMEM((2, tm, tf), BF16)],
        compiler_params=_params("parallel", "arbitrary"),
        name="conv_ffn",
    )(x, x, g_in.reshape(1, d), w_up, w_up, w_conv,
      b_conv.reshape(depth, 1, d_ff), w_down, g_out.reshape(1, d))


def _sb_kernel(q_ref, k_ref, v_ref, o_ref, acc_ref, carry_ref, tri_ref, *, tq,
               heads):
    i = pl.program_id(1)
    scale = HEAD_DIM ** -0.5
    half = SB_SEGMENT
    n_seg = tq // half
    hsl = [slice(h * HEAD_DIM, (h + 1) * HEAD_DIM) for h in range(heads)]

    @pl.when(i == 0)
    def _():
        rr = lax.broadcasted_iota(jnp.int32, (2 * half, 2 * half), 0)
        cc = lax.broadcasted_iota(jnp.int32, (2 * half, 2 * half), 1)
        tri_ref[...] = jnp.where((cc >= half) | ((rr & (half - 1)) > cc),
                                 1.0, 0.0).astype(BF16)

    tri = tri_ref[...]
    qi = lax.broadcasted_iota(jnp.int32, (half, half), 0)
    kj = lax.broadcasted_iota(jnp.int32, (half, half), 1)
    strictly_earlier = kj < qi

    def run(segments, nrows, nk, diagonal_last):
        chains = [(slice(r0, r0 + nrows), pl.ds(ks, nk), s)
                  for r0, ks in segments for s in hsl]
        n_half = nk // half
        last = (n_half - 1) * half

        def mask_last(t):
            if not diagonal_last:
                return t
            tail = jnp.where(strictly_earlier, t[:, last:], 0.0)
            return tail if n_half == 1 else jnp.concatenate(
                [t[:, :last], tail], axis=1)

        zs = [_dot_nt(q_ref[r, s], k_ref[k, s]) * scale for r, k, s in chains]
        softs = [jnp.log(1.0 + jnp.exp2(jnp.abs(z) * NEG_LOG2E)) for z in zs]
        log_sigs = [jnp.minimum(z, 0.0) - t for z, t in zip(zs, softs)]
        drops = [mask_last(jnp.maximum(z, 0.0) + t) for z, t in zip(zs, softs)]
        his = [d.astype(BF16) for d in drops]
        los = [(d - hi.astype(F32)).astype(BF16) for d, hi in zip(drops, his)]
        first = diagonal_last
        carries = [None if first else carry_ref[r, s] for r, _, s in chains]
        parts = [[None] * n_half for _ in chains]
        for c in reversed(range(n_half)):
            sl = slice(c * half, (c + 1) * half)
            ress = [_dot(jnp.concatenate([hi[:, sl], lo[:, sl]], axis=1), tri)
                    for hi, lo in zip(his, los)]
            for n, res in enumerate(ress):
                if carries[n] is None:
                    parts[n][c] = jnp.exp(log_sigs[n][:, sl] - res[:, :half])
                    carries[n] = res[:, half:]
                else:
                    parts[n][c] = jnp.exp(log_sigs[n][:, sl]
                                          - (carries[n] + res[:, :half]))
                    carries[n] = carries[n] + res[:, half:]
        for n, (r, k, s) in enumerate(chains):
            a = mask_last(jnp.concatenate(parts[n], axis=1))
            pv = _dot(a.astype(BF16), v_ref[k, s])
            if first:
                acc_ref[r, s] = pv
            else:
                acc_ref[r, s] += pv
            carry_ref[r, s] = carries[n]

    window = SB_HISTORY + half
    pairs = [list(range(s, min(s + 2, n_seg))) for s in range(0, n_seg, 2)]

    @pl.when(i == 0)
    def _():
        short = [s for s in range(n_seg) if s * half < SB_HISTORY]
        for s in short:
            run([(s * half, 0)], half, (s + 1) * half, True)
        rest = [(s * half, s * half - SB_HISTORY)
                for s in range(n_seg) if s not in short]
        if rest:
            run(rest, half, window, True)

    @pl.when(i > 0)
    def _():
        k0 = pl.multiple_of(i * tq - SB_HISTORY, half)
        for pair in pairs:
            run([(s * half, k0 + s * half) for s in pair], half, window, True)

    def live():
        return jnp.min(carry_ref[...]) < -EXP_UNDERFLOW_F32

    alive = live()

    @pl.when(alive & (i > 0))
    def _():
        k0 = pl.multiple_of(i * tq - SB_HISTORY, half)
        for s in range(1, n_seg):
            run([(s * half, k0)], half, s * half, False)

    @pl.when(alive & (i == 0))
    def _():
        for s in range(n_seg):
            unseen = s * half - SB_HISTORY
            if unseen > 0:
                run([(s * half, 0)], half, unseen, False)

    def cond(c):
        step, alive = c
        return (step < n_rest) & alive

    def body(c):
        step, _ = c
        kstart = pl.multiple_of(i * tq - SB_HISTORY * (step + 2), SB_HISTORY)
        run([(0, kstart)], tq, SB_HISTORY, False)
        return step + 1, live()

    n_rest = jnp.maximum((i * tq - SB_HISTORY) // SB_HISTORY, 0)
    lax.while_loop(cond, body, (jnp.int32(0), alive))
    o_ref[...] = acc_ref[...].astype(o_ref.dtype)


def _stick_breaking(kvq, *, tq, heads):
    s = kvq.shape[0]
    assert SB_SEGMENT == HEAD_DIM and SB_HISTORY % SB_SEGMENT == 0
    assert tq % SB_HISTORY == 0 and s % tq == 0 and SB_HEADS % heads == 0
    groups = SB_HEADS // heads
    width = heads * HEAD_DIM
    whole = functools.partial(pl.BlockSpec, (s, width),
                              pipeline_mode=pl.Buffered(1))
    return pl.pallas_call(
        functools.partial(_sb_kernel, tq=tq, heads=heads),
        grid=(groups, s // tq),
        in_specs=[pl.BlockSpec((tq, width), lambda g, i: (i, 2 * groups + g)),
                  whole(lambda g, i: (0, g)),
                  whole(lambda g, i: (0, groups + g))],
        out_specs=pl.BlockSpec((tq, width), lambda g, i: (i, g)),
        out_shape=jax.ShapeDtypeStruct((s, SB_W), BF16),
        scratch_shapes=[pltpu.VMEM((tq, width), F32),
                        pltpu.VMEM((tq, width), F32),
                        pltpu.VMEM((2 * SB_SEGMENT, 2 * SB_SEGMENT), BF16)],
        compiler_params=_params("arbitrary", "arbitrary"),
        name="stick_breaking",
    )(kvq, kvq, kvq)


def _rope_tables(s):
    pos = jnp.arange(s, dtype=F32)
    inv = ROPE_THETA ** (-jnp.arange(0, HEAD_DIM, 2, dtype=F32) / HEAD_DIM)
    ang = pos[:, None] * inv[None, :]
    cos, sin = jnp.cos(ang), jnp.sin(ang)
    return (jnp.concatenate([cos, cos], axis=-1),
            jnp.concatenate([-sin, sin], axis=-1))


def kernel(x, mem, norms, w_in_a, w_o_a, g_kv, w_kv, w_in_b, w_o_b, w_mem_kv,
           w_up, w_conv, b_conv, w_down):
    b, s, d = x.shape
    assert b == 1 and norms.shape[0] == 2
    xs = x.reshape(s, d)
    mems = mem.reshape(mem.shape[1], d)
    bf = lambda a: a.astype(BF16)
    proj_rows = min(PROJ_ROWS, s)
    out_rows = min(OUT_ROWS, s)
    ffn_rows = min(FFN_ROWS, s)

    nrm = norms[0]
    cos2, sin2 = _rope_tables(s)
    proj = _norm_matmul(xs, nrm[0:1], bf(w_in_a[0]), tm=proj_rows,
                        tn=PROJ_COLS, out_dtype=F32, rope_cols=2 * DIL_W,
                        cos2=cos2, sin2=sin2)
    mkv = _norm_matmul(mems, nrm[4:5], bf(w_mem_kv[0]), tm=mems.shape[0],
                       tn=GROUP_W, out_dtype=BF16)
    outs, lses = [], []
    for gi, (_, dilation) in enumerate(DIL_GROUPS):
        o_g, l_g = _dilated_group(proj, gi, dilation)
        outs.append(o_g)
        lses.append(l_g)
    xs = _outproj_merge(outs, lses, proj, mkv, xs, bf(w_o_a[0]), nrm[1],
                        tm=out_rows)
    w_up_bf, w_down_bf = bf(w_up), bf(w_down)
    xs = _ffn(xs, nrm[2], w_up_bf, w_conv, b_conv, w_down_bf, nrm[3], 0,
              tm=ffn_rows, tf=FFN_COLS)

    nrm = norms[1]
    kvq = _norm_matmul(xs, jnp.stack([g_kv, nrm[0]]),
                       bf(jnp.concatenate([w_kv, w_in_b[0]], axis=1)),
                       tm=proj_rows, tn=PROJ_COLS, out_dtype=BF16,
                       gain_split_col=2 * SB_W)
    mkv = _norm_matmul(mems, nrm[4:5], bf(w_mem_kv[1]), tm=mems.shape[0],
                       tn=GROUP_W, out_dtype=BF16)
    o_sb = _stick_breaking(kvq, tq=2 * SB_SEGMENT, heads=SB_HEADS_PER_STEP)
    xs = _outproj(o_sb, kvq, 3 * SB_W // MEM_W, mkv, xs, bf(w_o_b[0]), nrm[1],
                  tm=out_rows)
    xs = _ffn(xs, nrm[2], w_up_bf, w_conv, b_conv, w_down_bf, nrm[3], 1,
              tm=ffn_rows, tf=FFN_COLS)
    return xs.reshape(b, s, d)
```

```python
import functools

import jax
import jax.numpy as jnp
from jax import lax
from jax.experimental import pallas as pl
from jax.experimental.pallas import tpu as pltpu

HEAD_DIM = 128
MEM_HEADS = 4
DIL_GROUPS = ((128, 1), (512, 4), (2048, 16))
HEADS_PER_GROUP = 4
DIL_HEADS = len(DIL_GROUPS) * HEADS_PER_GROUP
SB_HEADS = 12
BLOCK = 128
CONV_WIDTH = 3
ROPE_THETA = 10000.0
EPS = 1e-6
NEG_INF = -1e30
EXP_UNDERFLOW_F32 = -105.0
NEG_LOG2E = -1.4426950408889634

DIL_W = DIL_HEADS * HEAD_DIM
MEM_W = MEM_HEADS * HEAD_DIM
SB_W = SB_HEADS * HEAD_DIM
GROUP_W = HEADS_PER_GROUP * HEAD_DIM
DIL_ROWS = 2 * BLOCK * max(d for _, d in DIL_GROUPS)
DIL_STAGE_CHAINS = {1: 16, 4: 16, 16: 8}
SB_SEGMENT = 128
SB_HISTORY = 256

F32 = jnp.float32
BF16 = jnp.bfloat16

VMEM_LIMIT_BYTES = 59 * 1024 * 1024
SUBLANE_BF16 = 16

PROJ_ROWS, PROJ_COLS = 1024, 1024
OUT_ROWS = 512
FFN_ROWS, FFN_COLS = 1024, 512
SB_HEADS_PER_STEP = 4


def _params(*sem):
    return pltpu.CompilerParams(dimension_semantics=sem,
                                vmem_limit_bytes=VMEM_LIMIT_BYTES)


def _rms(xf, g):
    y = xf * lax.rsqrt(jnp.mean(xf * xf, axis=-1, keepdims=True) + EPS)
    return y * g


def _dot(a, b):
    return jnp.dot(a, b, preferred_element_type=F32)


def _dot_nt(a, b):
    return lax.dot_general(a, b, (((1,), (1,)), ((), ())),
                           preferred_element_type=F32)


def _norm_rows(x_ref, g_ref, h_ref):
    xf = x_ref[...]
    y = xf * lax.rsqrt(jnp.mean(xf * xf, axis=-1, keepdims=True) + EPS)
    for gi in range(h_ref.shape[0]):
        h_ref[gi] = (y * g_ref[gi:gi + 1, :]).astype(BF16)


def _norm_matmul_kernel(*refs, rope_tiles, heads_per_tile, gain_split):
    if rope_tiles:
        x_ref, g_ref, w_ref, cos_ref, sin_ref, o_ref, h_ref = refs
    else:
        x_ref, g_ref, w_ref, o_ref, h_ref = refs
    j = pl.program_id(1)
    n_gains = h_ref.shape[0]

    @pl.when(j == 0)
    def _():
        _norm_rows(x_ref, g_ref, h_ref)

    if n_gains == 1:
        lhs = h_ref[0]
    else:
        lhs = h_ref[jnp.where(j >= gain_split, 1, 0)]
    acc = _dot(lhs, w_ref[...])

    if rope_tiles:
        is_rope = j < rope_tiles
        c = cos_ref[...]
        s = sin_ref[...]
        for hh in range(heads_per_tile):
            seg = acc[:, hh * HEAD_DIM:(hh + 1) * HEAD_DIM]
            rot = pltpu.roll(seg, HEAD_DIM // 2, 1)
            o_ref[hh] = jnp.where(is_rope, seg * c + rot * s,
                                  seg).astype(o_ref.dtype)
    else:
        o_ref[...] = acc.astype(o_ref.dtype)


def _norm_matmul(x, gains, w, *, tm, tn, out_dtype, gain_split_col=0,
                 rope_cols=0, cos2=None, sin2=None):
    m, k = x.shape
    n = w.shape[1]
    n_gains = gains.shape[0]
    assert m % tm == 0 and n % tn == 0
    assert rope_cols % tn == 0 and gain_split_col % tn == 0
    in_specs = [
        pl.BlockSpec((tm, k), lambda i, j: (i, 0)),
        pl.BlockSpec((n_gains, k), lambda i, j: (0, 0)),
        pl.BlockSpec((k, tn), lambda i, j: (0, j)),
    ]
    args = [x, gains, w]
    if rope_cols:
        in_specs += [pl.BlockSpec((tm, HEAD_DIM), lambda i, j: (i, 0))] * 2
        args += [cos2, sin2]
        out_spec = pl.BlockSpec((tn // HEAD_DIM, tm, HEAD_DIM),
                                lambda i, j: (j, i, 0))
        out_shape = jax.ShapeDtypeStruct((n // HEAD_DIM, m, HEAD_DIM), out_dtype)
    else:
        out_spec = pl.BlockSpec((tm, tn), lambda i, j: (i, j))
        out_shape = jax.ShapeDtypeStruct((m, n), out_dtype)
    return pl.pallas_call(
        functools.partial(_norm_matmul_kernel, rope_tiles=rope_cols // tn,
                          heads_per_tile=tn // HEAD_DIM,
                          gain_split=gain_split_col // tn),
        grid=(m // tm, n // tn),
        in_specs=in_specs,
        out_specs=out_spec,
        out_shape=out_shape,
        scratch_shapes=[pltpu.VMEM((n_gains, tm, k), BF16)],
        compiler_params=_params("parallel", "arbitrary"),
        name="norm_matmul",
    )(*args)


def _dilated_kernel(q_ref, kc_ref, vc_ref, kp_ref, vp_ref, o_ref, l_ref, *,
                    dilation, chains_per_stage):
    n = pl.program_id(0)
    scale = HEAD_DIM ** -0.5
    nq = DIL_ROWS // (BLOCK * dilation)
    qi = lax.broadcasted_iota(jnp.int32, (BLOCK, BLOCK), 0)
    kj = lax.broadcasted_iota(jnp.int32, (BLOCK, BLOCK), 1)
    band = kj >= qi
    valid_cur = kj <= qi

    def rows(ref, b, r):
        start = b * BLOCK * dilation + r
        if dilation == 1:
            return pl.ds(start, BLOCK)
        return pl.ds(start, BLOCK, stride=dilation)

    loaded = {}

    def kv(b, r):
        if (b, r) not in loaded:
            if b < 0:
                idx = rows(kp_ref, 0, r)
                loaded[(b, r)] = (kp_ref[idx, :].astype(BF16),
                                  vp_ref[idx, :].astype(BF16))
            else:
                idx = rows(kc_ref, b, r)
                loaded[(b, r)] = (kc_ref[idx, :].astype(BF16),
                                  vc_ref[idx, :].astype(BF16))
        return loaded[(b, r)]

    def run_stage(chains):
        qs, kps, kcs, vps, vcs, masks, idxs = [], [], [], [], [], [], []
        for b, r in chains:
            idx = rows(q_ref, b, r)
            qs.append(q_ref[idx, :].astype(BF16))
            kp, vp = kv(b - 1, r)
            kc, vc = kv(b, r)
            kps.append(kp); vps.append(vp); kcs.append(kc); vcs.append(vc)
            masks.append(band if b > 0 else band & (n > 0))
            idxs.append(idx)
        ks = [jnp.concatenate([kp, kc], axis=0) for kp, kc in zip(kps, kcs)]
        vs = [jnp.concatenate([vp, vc], axis=0) for vp, vc in zip(vps, vcs)]
        ms = [jnp.concatenate([m, valid_cur], axis=1) for m in masks]
        sc = [jnp.where(m, _dot_nt(q, k) * scale, NEG_INF)
              for q, k, m in zip(qs, ks, ms)]
        mx = [jnp.max(a, axis=-1, keepdims=True) for a in sc]
        tot = [jnp.sum(jnp.exp(a - m), axis=-1, keepdims=True)
               for a, m in zip(sc, mx)]
        lse = [m + jnp.log(t) for m, t in zip(mx, tot)]
        ps = [jnp.exp(a - l).astype(BF16) for a, l in zip(sc, lse)]
        for idx, p, v, l in zip(idxs, ps, vs, lse):
            o_ref[idx, :] = _dot(p, v)
            l_ref[idx, :] = jnp.broadcast_to(l, (BLOCK, HEAD_DIM))

    chains = [(b, r) for b in range(nq) for r in range(dilation)]
    for c0 in range(0, len(chains), chains_per_stage):
        run_stage(chains[c0:c0 + chains_per_stage])


def _dilated_group(proj, gi, dilation):
    s = proj.shape[1]
    assert s % DIL_ROWS == 0 and DIL_ROWS % (BLOCK * dilation) == 0
    prev_rows = BLOCK * dilation
    ratio = DIL_ROWS // prev_rows
    q_col = gi * HEADS_PER_GROUP
    k_col = DIL_HEADS + q_col
    v_col = 2 * DIL_HEADS + q_col

    def cur(col):
        return pl.BlockSpec((None, DIL_ROWS, HEAD_DIM),
                            lambda n, h: (col + h, n, 0))

    def prev(col):
        return pl.BlockSpec(
            (None, prev_rows, HEAD_DIM),
            lambda n, h: (col + h, jnp.maximum(n * ratio - 1, 0), 0))

    out_spec = pl.BlockSpec((None, DIL_ROWS, HEAD_DIM), lambda n, h: (h, n, 0))
    return pl.pallas_call(
        functools.partial(_dilated_kernel, dilation=dilation,
                          chains_per_stage=DIL_STAGE_CHAINS[dilation]),
        grid=(s // DIL_ROWS, HEADS_PER_GROUP),
        in_specs=[cur(q_col), cur(k_col), cur(v_col), prev(k_col),
                  prev(v_col)],
        out_specs=[out_spec, out_spec],
        out_shape=[jax.ShapeDtypeStruct((HEADS_PER_GROUP, s, HEAD_DIM), F32)] * 2,
        compiler_params=_params("parallel", "arbitrary"),
        name=f"dilated_d{dilation}",
    )(proj, proj, proj, proj, proj)


def _mem_attn(queries, mkv_ref):
    scale = HEAD_DIM ** -0.5
    ss = [_dot_nt(q.astype(BF16),
                  mkv_ref[:, hh * HEAD_DIM:(hh + 1) * HEAD_DIM]) * scale
          for hh, q in enumerate(queries)]
    es = [jnp.exp(s - jnp.max(s, axis=-1, keepdims=True)) for s in ss]
    ps = [(e / jnp.sum(e, axis=-1, keepdims=True)).astype(BF16) for e in es]
    return jnp.concatenate(
        [_dot(p, mkv_ref[:, MEM_W + hh * HEAD_DIM:
                         MEM_W + (hh + 1) * HEAD_DIM]).astype(BF16)
         for hh, p in enumerate(ps)], axis=1)


def _project_out(attn, o_mem, x_ref, w_ref, g_ref, out_ref):
    split = attn.shape[1]
    y = _dot(attn, w_ref[0:split, :]) + _dot(o_mem, w_ref[split:, :])
    out_ref[...] = x_ref[...] + _rms(y, g_ref[...])


def _outproj_merge_kernel(o1, o2, o3, l1, l2, l3, qm_ref, mkv_ref, x_ref, w_ref,
                          g_ref, out_ref):
    merged = []
    for hh in range(HEADS_PER_GROUP):
        a1, a2, a3 = l1[hh], l2[hh], l3[hh]
        mx = jnp.maximum(jnp.maximum(a1, a2), a3)
        e1, e2, e3 = jnp.exp(a1 - mx), jnp.exp(a2 - mx), jnp.exp(a3 - mx)
        den = e1 + e2 + e3
        o_dil = (e1 / den) * o1[hh] + (e2 / den) * o2[hh] + (e3 / den) * o3[hh]
        merged.append(o_dil.astype(BF16))
    o_mem = _mem_attn([qm_ref[hh] for hh in range(MEM_HEADS)], mkv_ref)
    _project_out(jnp.concatenate(merged, axis=1), o_mem, x_ref, w_ref, g_ref,
                 out_ref)


def _outproj_merge(outs, lses, proj, mkv, x, w_o, g, *, tm):
    s, d = x.shape
    row = lambda w: pl.BlockSpec((tm, w), lambda i: (i, 0))
    full = lambda a: pl.BlockSpec(a.shape, lambda i: (0, 0))
    heads = lambda n, blk: pl.BlockSpec((n, tm, HEAD_DIM),
                                        lambda i: (blk, i, 0))
    qm_block = proj.shape[0] // MEM_HEADS - 1
    g2 = g.reshape(1, d)
    return pl.pallas_call(
        _outproj_merge_kernel,
        grid=(s // tm,),
        in_specs=[heads(HEADS_PER_GROUP, 0)] * 6
        + [heads(MEM_HEADS, qm_block), full(mkv), row(d), full(w_o), full(g2)],
        out_specs=row(d),
        out_shape=jax.ShapeDtypeStruct((s, d), F32),
        compiler_params=_params("parallel"),
        name="outproj_merge",
    )(*outs, *lses, proj, mkv, x, w_o, g2)


def _outproj_kernel(oa_ref, qm_ref, mkv_ref, x_ref, w_ref, g_ref, out_ref):
    queries = [qm_ref[:, hh * HEAD_DIM:(hh + 1) * HEAD_DIM]
               for hh in range(MEM_HEADS)]
    _project_out(oa_ref[...], _mem_attn(queries, mkv_ref), x_ref, w_ref, g_ref,
                 out_ref)


def _outproj(o_attn, proj, qm_tile, mkv, x, w_o, g, *, tm):
    s, d = x.shape
    row = lambda w: pl.BlockSpec((tm, w), lambda i: (i, 0))
    full = lambda a: pl.BlockSpec(a.shape, lambda i: (0, 0))
    g2 = g.reshape(1, d)
    return pl.pallas_call(
        _outproj_kernel,
        grid=(s // tm,),
        in_specs=[row(o_attn.shape[1]),
                  pl.BlockSpec((tm, MEM_W), lambda i: (i, qm_tile)), full(mkv),
                  row(d), full(w_o), full(g2)],
        out_specs=row(d),
        out_shape=jax.ShapeDtypeStruct((s, d), F32),
        compiler_params=_params("parallel"),
        name="outproj",
    )(o_attn, proj, mkv, x, w_o, g2)


def _ffn_kernel(x_ref, xh_ref, gin_ref, wg_ref, wv_ref, wc_ref, bc_ref, wd_ref,
                gout_ref, o_ref, hn_ref, gbuf_ref, act_ref, *, tm, nf):
    i = pl.program_id(0)
    j = pl.program_id(1)
    halo = SUBLANE_BF16

    def up(slot):
        gate_all = _dot(hn_ref[...], wg_ref[...])
        val = _dot(hn_ref[halo:, :], wv_ref[...])
        gate = gate_all[halo:, :]
        gbuf_ref[0:halo, :] = jnp.where(i > 0, gate_all[0:halo, :], 0.0)
        gbuf_ref[halo:halo + tm, :] = gate
        g_m1 = gbuf_ref[pl.ds(halo - 1, tm), :]
        g_m2 = gbuf_ref[pl.ds(halo - 2, tm), :]
        acc = bc_ref[...] + gate * wc_ref[2:3, :]
        acc = acc + g_m2 * wc_ref[0:1, :]
        acc = acc + g_m1 * wc_ref[1:2, :]
        act = (acc * (1.0 / (1.0 + jnp.exp(-acc)))) * val
        act_ref[slot] = act.astype(BF16)

    def down(slot):
        o_ref[...] += _dot(act_ref[slot], wd_ref[...])

    @pl.when(j == 0)
    def _():
        hn_ref[0:halo, :] = _rms(xh_ref[...], gin_ref[...]).astype(BF16)
        hn_ref[halo:, :] = _rms(x_ref[...], gin_ref[...]).astype(BF16)
        o_ref[...] = jnp.zeros_like(o_ref)
        up(0)

    @pl.when((j > 0) & (j < nf))
    def _():
        slot = lax.rem(j, 2)
        down(1 - slot)
        up(slot)

    @pl.when(j == nf)
    def _():
        down((nf - 1) % 2)
        o_ref[...] = x_ref[...] + _rms(o_ref[...], gout_ref[...])


def _ffn(x, g_in, w_up, w_conv, b_conv, w_down, g_out, layer, *, tm, tf):
    s, d = x.shape
    depth, d_ff, _ = w_down.shape
    assert s % tm == 0 and d_ff % tf == 0 and tm % SUBLANE_BF16 == 0
    nf = d_ff // tf
    halo = SUBLANE_BF16
    up_j = lambda j: jnp.minimum(j, nf - 1)
    down_j = lambda j: jnp.maximum(j - 1, 0)
    return pl.pallas_call(
        functools.partial(_ffn_kernel, tm=tm, nf=nf),
        grid=(s // tm, nf + 1),
        in_specs=[
            pl.BlockSpec((tm, d), lambda i, j: (i, 0)),
            pl.BlockSpec((halo, d),
                         lambda i, j: (jnp.maximum(i * (tm // halo) - 1, 0), 0)),
            pl.BlockSpec((1, d), lambda i, j: (0, 0)),
            pl.BlockSpec((None, d, tf), lambda i, j: (layer, 0, up_j(j))),
            pl.BlockSpec((None, d, tf), lambda i, j: (layer, 0, nf + up_j(j))),
            pl.BlockSpec((None, CONV_WIDTH, tf),
                         lambda i, j: (layer, 0, up_j(j))),
            pl.BlockSpec((None, 1, tf), lambda i, j: (layer, 0, up_j(j))),
            pl.BlockSpec((None, tf, d), lambda i, j: (layer, down_j(j), 0)),
            pl.BlockSpec((1, d), lambda i, j: (0, 0)),
        ],
        out_specs=pl.BlockSpec((tm, d), lambda i, j: (i, 0)),
        out_shape=jax.ShapeDtypeStruct((s, d), F32),
        scratch_shapes=[pltpu.VMEM((halo + tm, d), BF16),
                        pltpu.VMEM((halo + tm, tf), F32),
                        pltpu.VMEM((2, tm, tf), BF16)],
        compiler_params=_params("parallel", "arbitrary"),
        name="conv_ffn",
    )(x, x, g_in.reshape(1, d), w_up, w_up, w_conv,
      b_conv.reshape(depth, 1, d_ff), w_down, g_out.reshape(1, d))


def _sb_kernel(q_ref, k_ref, v_ref, o_ref, acc_ref, carry_ref, tri_ref, *, tq,
               heads):
    i = pl.program_id(1)
    scale = HEAD_DIM ** -0.5
    half = SB_SEGMENT
    n_seg = tq // half
    hsl = [slice(h * HEAD_DIM, (h + 1) * HEAD_DIM) for h in range(heads)]

    @pl.when(i == 0)
    def _():
        rr = lax.broadcasted_iota(jnp.int32, (2 * half, 2 * half), 0)
        cc = lax.broadcasted_iota(jnp.int32, (2 * half, 2 * half), 1)
        tri_ref[...] = jnp.where((cc >= half) | ((rr & (half - 1)) > cc),
                                 1.0, 0.0).astype(BF16)

    tri = tri_ref[...]
    qi = lax.broadcasted_iota(jnp.int32, (half, half), 0)
    kj = lax.broadcasted_iota(jnp.int32, (half, half), 1)
    strictly_earlier = kj < qi

    def run(segments, nrows, nk, diagonal_last):
        chains = [(slice(r0, r0 + nrows), pl.ds(ks, nk), s)
                  for r0, ks in segments for s in hsl]
        n_half = nk // half
        last = (n_half - 1) * half

        def mask_last(t):
            if not diagonal_last:
                return t
            tail = jnp.where(strictly_earlier, t[:, last:], 0.0)
            return tail if n_half == 1 else jnp.concatenate(
                [t[:, :last], tail], axis=1)

        zs = [_dot_nt(q_ref[r, s], k_ref[k, s]) * scale for r, k, s in chains]
        softs = [jnp.log(1.0 + jnp.exp2(jnp.abs(z) * NEG_LOG2E)) for z in zs]
        log_sigs = [jnp.minimum(z, 0.0) - t for z, t in zip(zs, softs)]
        drops = [mask_last(jnp.maximum(z, 0.0) + t) for z, t in zip(zs, softs)]
        his = [d.astype(BF16) for d in drops]
        los = [(d - hi.astype(F32)).astype(BF16) for d, hi in zip(drops, his)]
        first = diagonal_last
        carries = [None if first else carry_ref[r, s] for r, _, s in chains]
        parts = [[None] * n_half for _ in chains]
        for c in reversed(range(n_half)):
            sl = slice(c * half, (c + 1) * half)
            ress = [_dot(jnp.concatenate([hi[:, sl], lo[:, sl]], axis=1), tri)
                    for hi, lo in zip(his, los)]
            for n, res in enumerate(ress):
                if carries[n] is None:
                    parts[n][c] = jnp.exp(log_sigs[n][:, sl] - res[:, :half])
                    carries[n] = res[:, half:]
                else:
                    parts[n][c] = jnp.exp(log_sigs[n][:, sl]
                                          - (carries[n] + res[:, :half]))
                    carries[n] = carries[n] + res[:, half:]
        for n, (r, k, s) in enumerate(chains):
            a = mask_last(jnp.concatenate(parts[n], axis=1))
            pv = _dot(a.astype(BF16), v_ref[k, s])
            if first:
                acc_ref[r, s] = pv
            else:
                acc_ref[r, s] += pv
            carry_ref[r, s] = carries[n]

    window = SB_HISTORY + half
    pairs = [list(range(s, min(s + 2, n_seg))) for s in range(0, n_seg, 2)]

    @pl.when(i == 0)
    def _():
        short = [s for s in range(n_seg) if s * half < SB_HISTORY]
        for s in short:
            run([(s * half, 0)], half, (s + 1) * half, True)
        rest = [(s * half, s * half - SB_HISTORY)
                for s in range(n_seg) if s not in short]
        if rest:
            run(rest, half, window, True)

    @pl.when(i > 0)
    def _():
        k0 = pl.multiple_of(i * tq - SB_HISTORY, half)
        for pair in pairs:
            run([(s * half, k0 + s * half) for s in pair], half, window, True)

    def live():
        return jnp.min(carry_ref[...]) < -EXP_UNDERFLOW_F32

    alive = live()

    @pl.when(alive & (i > 0))
    def _():
        k0 = pl.multiple_of(i * tq - SB_HISTORY, half)
        for s in range(1, n_seg):
            run([(s * half, k0)], half, s * half, False)

    @pl.when(alive & (i == 0))
    def _():
        for s in range(n_seg):
            unseen = s * half - SB_HISTORY
            if unseen > 0:
                run([(s * half, 0)], half, unseen, False)

    def cond(c):
        step, alive = c
        return (step < n_rest) & alive

    def body(c):
        step, _ = c
        kstart = pl.multiple_of(i * tq - SB_HISTORY * (step + 2), SB_HISTORY)
        run([(0, kstart)], tq, SB_HISTORY, False)
        return step + 1, live()

    n_rest = jnp.maximum((i * tq - SB_HISTORY) // SB_HISTORY, 0)
    lax.while_loop(cond, body, (jnp.int32(0), alive))
    o_ref[...] = acc_ref[...].astype(o_ref.dtype)


def _stick_breaking(kvq, *, tq, heads):
    s = kvq.shape[0]
    assert SB_SEGMENT == HEAD_DIM and SB_HISTORY % SB_SEGMENT == 0
    assert tq % SB_HISTORY == 0 and s % tq == 0 and SB_HEADS % heads == 0
    groups = SB_HEADS // heads
    width = heads * HEAD_DIM
    whole = functools.partial(pl.BlockSpec, (s, width),
                              pipeline_mode=pl.Buffered(1))
    return pl.pallas_call(
        functools.partial(_sb_kernel, tq=tq, heads=heads),
        grid=(groups, s // tq),
        in_specs=[pl.BlockSpec((tq, width), lambda g, i: (i, 2 * groups + g)),
                  whole(lambda g, i: (0, g)),
                  whole(lambda g, i: (0, groups + g))],
        out_specs=pl.BlockSpec((tq, width), lambda g, i: (i, g)),
        out_shape=jax.ShapeDtypeStruct((s, SB_W), BF16),
        scratch_shapes=[pltpu.VMEM((tq, width), F32),
                        pltpu.VMEM((tq, width), F32),
                        pltpu.VMEM((2 * SB_SEGMENT, 2 * SB_SEGMENT), BF16)],
        compiler_params=_params("arbitrary", "arbitrary"),
        name="stick_breaking",
    )(kvq, kvq, kvq)


def _rope_tables(s):
    pos = jnp.arange(s, dtype=F32)
    inv = ROPE_THETA ** (-jnp.arange(0, HEAD_DIM, 2, dtype=F32) / HEAD_DIM)
    ang = pos[:, None] * inv[None, :]
    cos, sin = jnp.cos(ang), jnp.sin(ang)
    return (jnp.concatenate([cos, cos], axis=-1),
            jnp.concatenate([-sin, sin], axis=-1))


def kernel(x, mem, norms, w_in_a, w_o_a, g_kv, w_kv, w_in_b, w_o_b, w_mem_kv,
           w_up, w_conv, b_conv, w_down):
    b, s, d = x.shape
    assert b == 1 and norms.shape[0] == 2
    xs = x.reshape(s, d)
    mems = mem.reshape(mem.shape[1], d)
    bf = lambda a: a.astype(BF16)
    proj_rows = min(PROJ_ROWS, s)
    out_rows = min(OUT_ROWS, s)
    ffn_rows = min(FFN_ROWS, s)

    nrm = norms[0]
    cos2, sin2 = _rope_tables(s)
    proj = _norm_matmul(xs, nrm[0:1], bf(w_in_a[0]), tm=proj_rows,
                        tn=PROJ_COLS, out_dtype=F32, rope_cols=2 * DIL_W,
                        cos2=cos2, sin2=sin2)
    mkv = _norm_matmul(mems, nrm[4:5], bf(w_mem_kv[0]), tm=mems.shape[0],
                       tn=GROUP_W, out_dtype=BF16)
    outs, lses = [], []
    for gi, (_, dilation) in enumerate(DIL_GROUPS):
        o_g, l_g = _dilated_group(proj, gi, dilation)
        outs.append(o_g)
        lses.append(l_g)
    xs = _outproj_merge(outs, lses, proj, mkv, xs, bf(w_o_a[0]), nrm[1],
                        tm=out_rows)
    w_up_bf, w_down_bf = bf(w_up), bf(w_down)
    xs = _ffn(xs, nrm[2], w_up_bf, w_conv, b_conv, w_down_bf, nrm[3], 0,
              tm=ffn_rows, tf=FFN_COLS)

    nrm = norms[1]
    kvq = _norm_matmul(xs, jnp.stack([g_kv, nrm[0]]),
                       bf(jnp.concatenate([w_kv, w_in_b[0]], axis=1)),
                       tm=proj_rows, tn=PROJ_COLS, out_dtype=BF16,
                       gain_split_col=2 * SB_W)
    mkv = _norm_matmul(mems, nrm[4:5], bf(w_mem_kv[1]), tm=mems.shape[0],
                       tn=GROUP_W, out_dtype=BF16)
    o_sb = _stick_breaking(kvq, tq=2 * SB_SEGMENT, heads=SB_HEADS_PER_STEP)
    xs = _outproj(o_sb, kvq, 3 * SB_W // MEM_W, mkv, xs, bf(w_o_b[0]), nrm[1],
                  tm=out_rows)
    xs = _ffn(xs, nrm[2], w_up_bf, w_conv, b_conv, w_down_bf, nrm[3], 1,
              tm=ffn_rows, tf=FFN_COLS)
    return xs.reshape(b, s, d)
```
